```python
import jax, jax.numpy as jnp
from jax import lax
import numpy as np

D_MODEL = 2048
BATCH = 16
SEQ = 2048
DEPTH = 4

D_MIX = D_MODEL
POOL_WIDTH = D_MIX // 4
POOL_WINDOWS = (2, 4, 8, 16)
POOL_GROUPS = len(POOL_WINDOWS)
POOL_GROUP = POOL_WIDTH // POOL_GROUPS
HGRN_WIDTH = D_MIX // 4
HGRN_HEAD_DIM = 128
HGRN_HEADS = HGRN_WIDTH // HGRN_HEAD_DIM
HGRN_CHUNK = 64
FOX_WIDTH = D_MIX - POOL_WIDTH - HGRN_WIDTH
FOX_HEAD_DIM = 128
FOX_HEADS = FOX_WIDTH // FOX_HEAD_DIM
FOX_BLOCK = 128
D_FF = 5632
CONV_WIDTH = 3
IN_SIZES = (POOL_WIDTH, HGRN_WIDTH, HGRN_WIDTH, HGRN_WIDTH, HGRN_WIDTH,
            FOX_WIDTH, FOX_WIDTH, FOX_WIDTH, FOX_HEADS)
IN_COLS = sum(IN_SIZES)
IN_SPLITS = tuple(int(v) for v in np.cumsum(IN_SIZES)[:-1])
DEEPNORM_ALPHA = (2 * DEPTH) ** 0.25
DEEPNORM_BETA = (8 * DEPTH) ** -0.25
LN_EPS = 1e-5
RMS_EPS = 1e-6
MASK_VALUE = -1e30
EXP_CLAMP = 80.0

kernel_name = 'hymba_pool_hgrn2_fox_deepnorm_trunk'


def layer_norm(x, g, b):
    xf = x.astype(jnp.float32)
    mu = jnp.mean(xf, axis=-1, keepdims=True)
    var = jnp.mean(jnp.square(xf - mu), axis=-1, keepdims=True)
    return ((xf - mu) * lax.rsqrt(var + LN_EPS)).astype(x.dtype) * g + b


def pool_mixer(u, pool_w, pool_scale):
    B_, S_, _ = u.shape
    max_w = POOL_WINDOWS[-1]
    uf = u.astype(jnp.float32)
    csum = jnp.pad(jnp.cumsum(uf, axis=1), ((0, 0), (max_w, 0), (0, 0)))
    t = jnp.arange(S_, dtype=jnp.float32)
    means = []
    for gi, w in enumerate(POOL_WINDOWS):
        c = csum[:, :, gi * POOL_GROUP:(gi + 1) * POOL_GROUP]
        window_sum = c[:, max_w:] - c[:, max_w - w:max_w - w + S_]
        count = jnp.minimum(t + 1.0, float(w))
        means.append(window_sum / count[None, :, None])
    pooled = jnp.stack(means, axis=2)
    d = (pooled - uf.reshape(B_, S_, POOL_GROUPS, POOL_GROUP)).astype(u.dtype)
    y = jnp.einsum('bsgc,gcd->bsgd', d, pool_w).reshape(B_, S_, POOL_WIDTH)
    return y * pool_scale


def hgrn2_mixer(q, f_logit, i, g, lb, norm_g):
    B_, S_, _ = q.shape
    n_chunks = S_ // HGRN_CHUNK
    f32 = jnp.float32
    z = f_logit.astype(f32)
    lb = lb.astype(f32)
    log_f = jax.nn.log_sigmoid(z) + jnp.log1p(lb * jnp.exp(jnp.minimum(-z, EXP_CLAMP)))
    log_f = jnp.minimum(log_f, 0.0)
    k = (1.0 - lb) * jax.nn.sigmoid(-z)

    def to_chunks(a):
        return a.astype(f32).reshape(B_, n_chunks, HGRN_CHUNK, HGRN_HEADS, HGRN_HEAD_DIM).transpose(1, 0, 3, 2, 4)

    qc, kc, vc = to_chunks(q), to_chunks(k), to_chunks(i)
    bc = jnp.cumsum(to_chunks(log_f), axis=3)
    causal = jnp.tril(jnp.ones((HGRN_CHUNK, HGRN_CHUNK), dtype=bool))[:, :, None]

    def chunk_step(state, xs):
        qb, kb, vb, bb = xs
        o_inter = jnp.einsum('bhtd,bhde->bhte', qb * jnp.exp(bb), state)
        rel = bb[:, :, :, None, :] - bb[:, :, None, :, :]
        decay = jnp.where(causal, jnp.exp(jnp.where(causal, rel, 0.0)), 0.0)
        att = jnp.einsum('bhtd,bhsd,bhtsd->bhts', qb, kb, decay)
        o_intra = jnp.einsum('bhts,bhse->bhte', att, vb)
        b_last = bb[:, :, -1:, :]
        new_state = state * jnp.exp(b_last[:, :, 0, :, None]) + jnp.einsum(
            'bhsd,bhse->bhde', kb * jnp.exp(b_last - bb), vb)
        return new_state, o_inter + o_intra

    state0 = jnp.zeros((B_, HGRN_HEADS, HGRN_HEAD_DIM, HGRN_HEAD_DIM), f32)
    _, o = lax.scan(chunk_step, state0, (qc, kc, vc, bc))
    o = o.transpose(1, 0, 3, 2, 4).reshape(B_, S_, HGRN_HEADS, HGRN_HEAD_DIM)
    o = o * lax.rsqrt(jnp.mean(o * o, axis=-1, keepdims=True) + RMS_EPS)
    o = o.reshape(B_, S_, HGRN_WIDTH) * norm_g.astype(f32) * jax.nn.silu(g.astype(f32))
    return o.astype(q.dtype)


def fox_attention(q, k, v, f_logit, f_bias):
    B_, S_, _ = q.shape
    q = q.reshape(B_, S_, FOX_HEADS, FOX_HEAD_DIM)
    k = k.reshape(B_, S_, FOX_HEADS, FOX_HEAD_DIM)
    v = v.reshape(B_, S_, FOX_HEADS, FOX_HEAD_DIM)
    log_f = jax.nn.log_sigmoid((f_logit + f_bias).astype(jnp.float32))
    cum = jnp.cumsum(log_f, axis=1).transpose(0, 2, 1)
    scale = FOX_HEAD_DIM ** -0.5
    outs = []
    for blk in range(S_ // FOX_BLOCK):
        t0, t1 = blk * FOX_BLOCK, (blk + 1) * FOX_BLOCK
        s = jnp.einsum('bthd,bshd->bhts', q[:, t0:t1], k[:, :t1]).astype(jnp.float32) * scale
        s = s + cum[:, :, t0:t1, None] - cum[:, :, None, :t1]
        mask = (t0 + jnp.arange(FOX_BLOCK))[:, None] >= jnp.arange(t1)[None, :]
        p = jax.nn.softmax(jnp.where(mask, s, MASK_VALUE), axis=-1).astype(v.dtype)
        outs.append(jnp.einsum('bhts,bshd->bthd', p, v[:, :t1]))
    return jnp.concatenate(outs, axis=1).reshape(B_, S_, FOX_WIDTH)


def conv_ffn(x, w_gate, w_val, conv_w, conv_b, w_down):
    gate = jnp.einsum('bsd,df->bsf', x, w_gate)
    val = jnp.einsum('bsd,df->bsf', x, w_val)
    gate = lax.conv_general_dilated(
        gate, conv_w[:, None, :], window_strides=(1,), padding=[(CONV_WIDTH - 1, 0)],
        dimension_numbers=('NWC', 'WIO', 'NWC'), feature_group_count=D_FF) + conv_b
    return jnp.einsum('bsf,fd->bsd', jax.nn.silu(gate) * val, w_down)


def setup_inputs(seed: int = 0) -> dict:
    key = jax.random.key(seed)
    ks = jax.random.split(key, 17)
    n = jax.random.normal
    f32 = jnp.float32
    return {
        'x': n(ks[0], (BATCH, SEQ, D_MODEL), f32),
        'w_in': n(ks[1], (DEPTH, D_MODEL, IN_COLS), f32) * D_MODEL ** -0.5,
        'fox_f_bias': 0.01 * n(ks[2], (DEPTH, FOX_HEADS), f32),
        'pool_w': n(ks[3], (DEPTH, POOL_GROUPS, POOL_GROUP, POOL_GROUP), f32) * POOL_GROUP ** -0.5,
        'pool_scale': 1.0 + 0.02 * n(ks[4], (DEPTH, POOL_WIDTH), f32),
        'hgrn_lb_logits': 0.1 * n(ks[5], (DEPTH, HGRN_WIDTH), f32),
        'hgrn_norm_g': 1.0 + 0.02 * n(ks[6], (DEPTH, HGRN_WIDTH), f32),
        'w_out': n(ks[7], (DEPTH, D_MIX, D_MODEL), f32) * (D_MIX ** -0.5 * DEEPNORM_BETA),
        'ln1_g': 1.0 + 0.02 * n(ks[8], (DEPTH, D_MODEL), f32),
        'ln1_b': 0.01 * n(ks[9], (DEPTH, D_MODEL), f32),
        'w_gate': n(ks[10], (DEPTH, D_MODEL, D_FF), f32) * D_MODEL ** -0.5,
        'w_val': n(ks[11], (DEPTH, D_MODEL, D_FF), f32) * D_MODEL ** -0.5,
        'conv_w': n(ks[12], (DEPTH, CONV_WIDTH, D_FF), f32) * CONV_WIDTH ** -0.5,
        'conv_b': 0.01 * n(ks[13], (DEPTH, D_FF), f32),
        'w_down': n(ks[14], (DEPTH, D_FF, D_MODEL), f32) * (D_FF ** -0.5 * DEEPNORM_BETA),
        'ln2_g': 1.0 + 0.02 * n(ks[15], (DEPTH, D_MODEL), f32),
        'ln2_b': 0.01 * n(ks[16], (DEPTH, D_MODEL), f32),
    }


def reference(x, w_in, fox_f_bias, pool_w, pool_scale, hgrn_lb_logits, hgrn_norm_g, w_out,
              ln1_g, ln1_b, w_gate, w_val, conv_w, conv_b, w_down, ln2_g, ln2_b):
    lb_w = jax.nn.softmax(hgrn_lb_logits.astype(jnp.float32), axis=0)
    lower_bounds = jnp.cumsum(lb_w, axis=0) - lb_w[0:1]
    for l in range(DEPTH):
        h = jnp.einsum('bsd,de->bse', x, w_in[l])
        u_pool, hq, hf, hi, hg, fq, fk, fv, ff = jnp.split(h, IN_SPLITS, axis=-1)
        y = jnp.concatenate([
            pool_mixer(u_pool, pool_w[l], pool_scale[l]),
            hgrn2_mixer(hq, hf, hi, hg, lower_bounds[l], hgrn_norm_g[l]),
            fox_attention(fq, fk, fv, ff, fox_f_bias[l]),
        ], axis=-1)
        x = layer_norm(DEEPNORM_ALPHA * x + jnp.einsum('bse,ed->bsd', y, w_out[l]), ln1_g[l], ln1_b[l])
        x = layer_norm(DEEPNORM_ALPHA * x + conv_ffn(x, w_gate[l], w_val[l], conv_w[l], conv_b[l], w_down[l]),
                       ln2_g[l], ln2_b[l])
    return x
```

```python
import functools

import jax
import jax.numpy as jnp
import numpy as np
from jax import lax
from jax.experimental import pallas as pl
from jax.experimental.pallas import tpu as pltpu

F32 = jnp.float32
BF16 = jnp.bfloat16

D_MODEL = 2048
DEPTH = 4
POOL_WIDTH = 512
POOL_WINDOWS = (2, 4, 8, 16)
POOL_GROUP = 128
POOL_HALO = 16
HGRN_WIDTH = 512
HGRN_HEAD_DIM = 128
HGRN_HEADS = 4
FOX_WIDTH = 1024
FOX_HEAD_DIM = 128
FOX_HEADS = 8
D_FF = 5632
DEEPNORM_ALPHA = (2 * DEPTH) ** 0.25
LN_EPS = 1e-5
RMS_EPS = 1e-6
MASK_VALUE = -1e30
EXP_CLAMP = 80.0

H32_COLS = POOL_WIDTH + 4 * HGRN_WIDTH
H16_COLS = 3 * FOX_WIDTH
FF_PAD = 128

LANE = 128
VMEM_LIMIT = 56 * 1024 * 1024

TM_INPROJ = 512
TC_CUMSUM = 256
TP_POOL = 256
TS_HGRN = 256
TQ_FOX = 512
TM_OUTPROJ = 256
TM_FFN_UP = 1024
TN_FFN_UP = 512
TM_FFN_DOWN = 256

HGRN_CHUNK = 64
HGRN_LEVELS = (32, 16, 8, 4, 2, 1)


def _params(sem):
    return pltpu.CompilerParams(dimension_semantics=sem, vmem_limit_bytes=VMEM_LIMIT)


def _log_sigmoid(x):
    return jnp.minimum(x, 0.0) - jnp.log1p(jnp.exp(-jnp.abs(x)))


def _split3(x):
    a = x.astype(BF16)
    r = x - a.astype(F32)
    b = r.astype(BF16)
    c = (r - b.astype(F32)).astype(BF16)
    return jnp.concatenate([a, b, c], axis=1)


def _sum3(y, n):
    return y[:, 0:n] + y[:, n:2 * n] + y[:, 2 * n:3 * n]


def _inproj32_kernel(x_ref, w_ref, o_ref):
    o_ref[...] = jnp.dot(x_ref[...], w_ref[...], preferred_element_type=F32)


def _inproj16_kernel(x_ref, w_ref, wff_ref, o_ref, ff_ref):
    x = x_ref[...]
    o_ref[...] = jnp.dot(x, w_ref[...], preferred_element_type=F32).astype(BF16)
    ff_ref[...] = jnp.dot(x, wff_ref[...], preferred_element_type=F32)


def _inproj(xb, w32, w16, wff):
    m = xb.shape[0]
    tm = TM_INPROJ
    h32 = pl.pallas_call(
        _inproj32_kernel,
        grid=(m // tm,),
        in_specs=[pl.BlockSpec((tm, D_MODEL), lambda i: (i, 0)),
                  pl.BlockSpec((D_MODEL, H32_COLS), lambda i: (0, 0))],
        out_specs=pl.BlockSpec((tm, H32_COLS), lambda i: (i, 0)),
        out_shape=jax.ShapeDtypeStruct((m, H32_COLS), F32),
        compiler_params=_params(("arbitrary",)),
        name="inproj32",
    )(xb, w32)
    h16, ff = pl.pallas_call(
        _inproj16_kernel,
        grid=(m // tm,),
        in_specs=[pl.BlockSpec((tm, D_MODEL), lambda i: (i, 0)),
                  pl.BlockSpec((D_MODEL, H16_COLS), lambda i: (0, 0)),
                  pl.BlockSpec((D_MODEL, FF_PAD), lambda i: (0, 0))],
        out_specs=[pl.BlockSpec((tm, H16_COLS), lambda i: (i, 0)),
                   pl.BlockSpec((tm, FF_PAD), lambda i: (i, 0))],
        out_shape=[jax.ShapeDtypeStruct((m, H16_COLS), BF16),
                   jax.ShapeDtypeStruct((m, FF_PAD), F32)],
        compiler_params=_params(("arbitrary",)),
        name="inproj16",
    )(xb, w16, wff)
    return h32, h16, ff


def _cum_kernel(ff_ref, bias_ref, cum_ref, cumt_ref, carry_ref):
    tc = ff_ref.shape[0]

    @pl.when(pl.program_id(1) == 0)
    def _():
        carry_ref[...] = jnp.zeros_like(carry_ref)

    lf = _log_sigmoid(ff_ref[...] + bias_ref[...])
    row = lax.broadcasted_iota(jnp.int32, (tc, tc), 0)
    col = lax.broadcasted_iota(jnp.int32, (tc, tc), 1)
    tril = jnp.where(row >= col, 1.0, 0.0).astype(BF16)
    c = _sum3(jnp.dot(tril, _split3(lf), preferred_element_type=F32), LANE) + carry_ref[...]
    cum_ref[...] = c
    cumt_ref[...] = c.T[0:FOX_HEADS, :]
    carry_ref[...] = c[tc - 1:tc, :]


def _fox_cumsum(ff, bias_pad, batch, seq):
    tc = TC_CUMSUM
    nt = seq // tc
    return pl.pallas_call(
        _cum_kernel,
        grid=(batch, nt),
        in_specs=[pl.BlockSpec((tc, FF_PAD), lambda b, i: (b * nt + i, 0)),
                  pl.BlockSpec((1, FF_PAD), lambda b, i: (0, 0))],
        out_specs=[pl.BlockSpec((tc, FF_PAD), lambda b, i: (b * nt + i, 0)),
                   pl.BlockSpec((None, FOX_HEADS, tc), lambda b, i: (b, 0, i))],
        out_shape=[jax.ShapeDtypeStruct((batch * seq, FF_PAD), F32),
                   jax.ShapeDtypeStruct((batch, FOX_HEADS, seq), F32)],
        scratch_shapes=[pltpu.VMEM((1, FF_PAD), F32)],
        compiler_params=_params(("arbitrary", "arbitrary")),
        name="fox_cumsum",
    )(ff, bias_pad)


def _pool_kernel(u_ref, pw_ref, ps_ref, o_ref, ext_ref):
    tp = u_ref.shape[0]
    i = pl.program_id(1)

    @pl.when(i == 0)
    def _():
        ext_ref[0:POOL_HALO, :] = jnp.zeros((POOL_HALO, POOL_WIDTH), F32)

    u = u_ref[...]
    ext_ref[POOL_HALO:POOL_HALO + tp, :] = u
    pos = (i * tp + lax.broadcasted_iota(jnp.int32, (tp, 1), 0)).astype(F32)
    for gi, w in enumerate(POOL_WINDOWS):
        lanes = slice(gi * POOL_GROUP, (gi + 1) * POOL_GROUP)
        ug = u[:, lanes]
        acc = ug
        for j in range(1, w):
            acc = acc + ext_ref[POOL_HALO - j:POOL_HALO - j + tp, lanes]
        count = jnp.minimum(pos + 1.0, float(w))
        d = (acc / count - ug).astype(BF16)
        y = jnp.dot(d, pw_ref[gi], preferred_element_type=F32) * ps_ref[:, lanes]
        o_ref[:, lanes] = y.astype(BF16)
    ext_ref[0:POOL_HALO, :] = ext_ref[tp:tp + POOL_HALO, :]


def _pool(h32, pool_w, pool_scale, batch, seq):
    tp = TP_POOL
    nt = seq // tp
    return pl.pallas_call(
        _pool_kernel,
        grid=(batch, nt),
        in_specs=[pl.BlockSpec((tp, POOL_WIDTH), lambda b, i: (b * nt + i, 0)),
                  pl.BlockSpec((len(POOL_WINDOWS), POOL_GROUP, POOL_GROUP), lambda b, i: (0, 0, 0)),
                  pl.BlockSpec((1, POOL_WIDTH), lambda b, i: (0, 0))],
        out_specs=pl.BlockSpec((tp, POOL_WIDTH), lambda b, i: (b * nt + i, 0)),
        out_shape=jax.ShapeDtypeStruct((batch * seq, POOL_WIDTH), BF16),
        scratch_shapes=[pltpu.VMEM((tp + POOL_HALO, POOL_WIDTH), F32)],
        compiler_params=_params(("arbitrary", "arbitrary")),
        name="pool_mixer",
    )(h32, pool_w, pool_scale)


def _hgrn_level_matrix():
    c = HGRN_CHUNK
    t = np.arange(c)[:, None]
    u = np.arange(c)[None, :]
    mats = [(u <= t)]
    for m in HGRN_LEVELS:
        p = t % (2 * m)
        r = t - p + m - 1
        upper = p >= m
        mats.append(np.where(upper, (u > r) & (u <= t), (u > t) & (u <= r)))
    return np.concatenate(mats, axis=0).astype(np.float32)


def _hgrn_kernel(q_ref, z_ref, v_ref, g_ref, lbl_ref, ng_ref, gm_ref, o_ref, st_ref, *, layer):
    c = HGRN_CHUNK
    d = HGRN_HEAD_DIM
    n_chunks = q_ref.shape[0] // c

    @pl.when(pl.program_id(1) == 0)
    def _():
        st_ref[...] = jnp.zeros_like(st_ref)

    lbl = lbl_ref[...]
    e = jnp.exp(lbl - jnp.max(lbl, axis=0, keepdims=True))
    lbw = e / jnp.sum(e, axis=0, keepdims=True)
    lb_all = jnp.zeros((1, HGRN_WIDTH), F32)
    for li in range(1, layer + 1):
        lb_all = lb_all + lbw[li:li + 1, :]

    gm = gm_ref[...]
    row_d = lax.broadcasted_iota(jnp.int32, (c, d), 0)
    row_c = lax.broadcasted_iota(jnp.int32, (c, c), 0)
    col_c = lax.broadcasted_iota(jnp.int32, (c, c), 1)
    same = row_c ^ col_c
    diag_mask = row_c == col_c
    dn_t = (((1,), (1,)), ((), ()))
    dn_a = (((0,), (0,)), ((), ()))

    def chunk_body(ci, carry):
        r0 = pl.multiple_of(ci * c, c)
        for h in range(HGRN_HEADS):
            lanes = slice(h * d, (h + 1) * d)
            lb = lb_all[:, lanes]
            q = q_ref[pl.ds(r0, c), lanes]
            z = z_ref[pl.ds(r0, c), lanes]
            v = v_ref[pl.ds(r0, c), lanes].astype(BF16)
            g = g_ref[pl.ds(r0, c), lanes]

            log_f = _log_sigmoid(z) + jnp.log1p(lb * jnp.exp(jnp.minimum(-z, EXP_CLAMP)))
            log_f = jnp.minimum(log_f, 0.0)
            kk = (1.0 - lb) / (1.0 + jnp.exp(z))

            ex = _sum3(jnp.dot(gm, _split3(log_f), preferred_element_type=F32), d)
            b = ex[0:c]
            b_last = b[c - 1:c, :]
            st = st_ref[h]

            qi = (q * jnp.exp(b)).astype(BF16)
            o = lax.dot_general(qi, st.astype(BF16), dn_t, preferred_element_type=F32)
            kd = (kk * jnp.exp(b_last - b)).astype(BF16)
            st_ref[h] = st * jnp.exp(b_last) + lax.dot_general(v, kd, dn_a, preferred_element_type=F32)

            att = jnp.where(
                diag_mask,
                lax.dot_general(q.astype(BF16), kk.astype(BF16), dn_t, preferred_element_type=F32),
                0.0)
            for li, m in enumerate(HGRN_LEVELS):
                upper = (row_d & m) != 0
                x = jnp.where(upper, q, kk) * jnp.exp(jnp.minimum(ex[(li + 1) * c:(li + 2) * c], 0.0))
                qm = jnp.where(upper, x, 0.0).astype(BF16)
                km = jnp.where(upper, 0.0, x).astype(BF16)
                a = lax.dot_general(qm, km, dn_t, preferred_element_type=F32)
                mask = (same < 2 * m) & ((row_c & m) != 0) & ((col_c & m) == 0)
                att = att + jnp.where(mask, a, 0.0)
            o = o + jnp.dot(att.astype(BF16), v, preferred_element_type=F32)

            o = o * lax.rsqrt(jnp.mean(o * o, axis=-1, keepdims=True) + RMS_EPS)
            o = o * ng_ref[:, lanes] * (g / (1.0 + jnp.exp(-g)))
            o_ref[pl.ds(r0, c), lanes] = o.astype(BF16)
        return carry

    lax.fori_loop(0, n_chunks, chunk_body, 0)


def _hgrn(h32, lb_logits, norm_g, layer, batch, seq):
    ts = TS_HGRN
    nt = seq // ts
    gm = jnp.asarray(_hgrn_level_matrix(), dtype=BF16)

    def col(k):
        return pl.BlockSpec((ts, HGRN_WIDTH), lambda b, i: (b * nt + i, k))

    return pl.pallas_call(
        functools.partial(_hgrn_kernel, layer=layer),
        grid=(batch, nt),
        in_specs=[col(1), col(2), col(3), col(4),
                  pl.BlockSpec((DEPTH, HGRN_WIDTH), lambda b, i: (0, 0)),
                  pl.BlockSpec((1, HGRN_WIDTH), lambda b, i: (0, 0)),
                  pl.BlockSpec(gm.shape, lambda b, i: (0, 0))],
        out_specs=pl.BlockSpec((ts, HGRN_WIDTH), lambda b, i: (b * nt + i, 0)),
        out_shape=jax.ShapeDtypeStruct((batch * seq, HGRN_WIDTH), BF16),
        scratch_shapes=[pltpu.VMEM((HGRN_HEADS, HGRN_HEAD_DIM, HGRN_HEAD_DIM), F32)],
        compiler_params=_params(("arbitrary", "arbitrary")),
        name="hgrn_mixer",
    )(h32, h32, h32, h32, lb_logits, norm_g, gm)


def _fox_kernel(q_ref, k_ref, v_ref, cq_ref, ck_ref, o_ref, m_ref, l_ref, acc_ref):
    tq = q_ref.shape[0]
    tk = k_ref.shape[0]
    i = pl.program_id(1)
    j = pl.program_id(2)
    dh = FOX_HEAD_DIM
    scale = FOX_HEAD_DIM ** -0.5
    dn_t = (((1,), (1,)), ((), ()))

    @pl.when(j == 0)
    def _():
        m_ref[...] = jnp.full(m_ref.shape, MASK_VALUE, F32)
        l_ref[...] = jnp.zeros_like(l_ref)
        acc_ref[...] = jnp.zeros_like(acc_ref)

    @pl.when(j <= i)
    def _():
        t_idx = i * tq + lax.broadcasted_iota(jnp.int32, (tq, tk), 0)
        s_idx = j * tk + lax.broadcasted_iota(jnp.int32, (tq, tk), 1)
        causal = t_idx >= s_idx
        for h in range(FOX_HEADS):
            lanes = slice(h * dh, (h + 1) * dh)
            s = lax.dot_general(q_ref[:, lanes], k_ref[:, lanes], dn_t, preferred_element_type=F32)
            s = s * scale + (cq_ref[:, h:h + 1] - ck_ref[h:h + 1, :])
            s = jnp.where(causal, s, MASK_VALUE)
            m_prev = m_ref[h]
            m_new = jnp.maximum(m_prev, jnp.max(s, axis=-1, keepdims=True))
            alpha = jnp.exp(m_prev - m_new)
            p = jnp.exp(s - m_new)
            l_ref[h] = alpha * l_ref[h] + jnp.sum(p, axis=-1, keepdims=True)
            acc_ref[h] = alpha * acc_ref[h] + jnp.dot(p.astype(BF16), v_ref[:, lanes],
                                                      preferred_element_type=F32)
            m_ref[h] = m_new

    @pl.when(j == i)
    def _():
        for h in range(FOX_HEADS):
            lanes = slice(h * dh, (h + 1) * dh)
            o_ref[:, lanes] = (acc_ref[h] / l_ref[h]).astype(BF16)


def _fox(h16, cum, cumt, batch, seq):
    tq = TQ_FOX
    nq = seq // tq
    return pl.pallas_call(
        _fox_kernel,
        grid=(batch, nq, nq),
        in_specs=[pl.BlockSpec((tq, FOX_WIDTH), lambda b, i, j: (b * nq + i, 0)),
                  pl.BlockSpec((tq, FOX_WIDTH), lambda b, i, j: (b * nq + jnp.minimum(j, i), 1)),
                  pl.BlockSpec((tq, FOX_WIDTH), lambda b, i, j: (b * nq + jnp.minimum(j, i), 2)),
                  pl.BlockSpec((tq, FF_PAD), lambda b, i, j: (b * nq + i, 0)),
                  pl.BlockSpec((None, FOX_HEADS, tq), lambda b, i, j: (b, 0, jnp.minimum(j, i)))],
        out_specs=pl.BlockSpec((tq, FOX_WIDTH), lambda b, i, j: (b * nq + i, 0)),
        out_shape=jax.ShapeDtypeStruct((batch * seq, FOX_WIDTH), BF16),
        scratch_shapes=[pltpu.VMEM((FOX_HEADS, tq, 1), F32),
                        pltpu.VMEM((FOX_HEADS, tq, 1), F32),
                        pltpu.VMEM((FOX_HEADS, tq, FOX_HEAD_DIM), F32)],
        compiler_params=_params(("arbitrary", "arbitrary", "arbitrary")),
        name="fox_attention",
    )(h16, h16, h16, cum, cumt)


def _layer_norm(z, g, b):
    mu = jnp.mean(z, axis=-1, keepdims=True)
    zc = z - mu
    var = jnp.mean(zc * zc, axis=-1, keepdims=True)
    return zc * lax.rsqrt(var + LN_EPS) * g + b


def _outproj_kernel(yp_ref, yh_ref, yf_ref, w_ref, x_ref, g_ref, b_ref, o32_ref, o16_ref):
    y = jnp.concatenate([yp_ref[...], yh_ref[...], yf_ref[...]], axis=1)
    z = DEEPNORM_ALPHA * x_ref[...] + jnp.dot(y, w_ref[...], preferred_element_type=F32)
    out = _layer_norm(z, g_ref[...], b_ref[...])
    o32_ref[...] = out
    o16_ref[...] = out.astype(BF16)


def _outproj(yp, yh, yf, w, x, g, b):
    m = x.shape[0]
    tm = TM_OUTPROJ
    row = lambda width: pl.BlockSpec((tm, width), lambda i: (i, 0))
    const = lambda shape: pl.BlockSpec(shape, lambda i: (0, 0))
    return pl.pallas_call(
        _outproj_kernel,
        grid=(m // tm,),
        in_specs=[row(POOL_WIDTH), row(HGRN_WIDTH), row(FOX_WIDTH), const((D_MODEL, D_MODEL)),
                  row(D_MODEL), const((1, D_MODEL)), const((1, D_MODEL))],
        out_specs=[row(D_MODEL), row(D_MODEL)],
        out_shape=[jax.ShapeDtypeStruct((m, D_MODEL), F32), jax.ShapeDtypeStruct((m, D_MODEL), BF16)],
        compiler_params=_params(("arbitrary",)),
        name="outproj_ln",
    )(yp, yh, yf, w, x, g, b)


def _ffn_up_kernel(x_ref, wg_ref, wv_ref, cw_ref, cb_ref, o_ref, gbuf_ref, *, tiles_per_seq):
    tm = x_ref.shape[0]
    i = pl.program_id(1)
    x = x_ref[...]
    gate = jnp.dot(x, wg_ref[...], preferred_element_type=F32)
    val = jnp.dot(x, wv_ref[...], preferred_element_type=F32)

    @pl.when(i % tiles_per_seq == 0)
    def _():
        gbuf_ref[0:8, :] = jnp.zeros((8, gbuf_ref.shape[1]), F32)

    gbuf_ref[8:8 + tm, :] = gate
    conv = (cw_ref[2:3, :] * gate + cw_ref[1:2, :] * gbuf_ref[7:7 + tm, :]
            + cw_ref[0:1, :] * gbuf_ref[6:6 + tm, :] + cb_ref[...])
    o_ref[...] = (conv / (1.0 + jnp.exp(-conv)) * val).astype(BF16)
    gbuf_ref[0:8, :] = gbuf_ref[tm:tm + 8, :]


def _ffn_up(xb, wg, wv, cw, cb, seq):
    m = xb.shape[0]
    tm, tn = TM_FFN_UP, TN_FFN_UP
    return pl.pallas_call(
        functools.partial(_ffn_up_kernel, tiles_per_seq=seq // tm),
        grid=(D_FF // tn, m // tm),
        in_specs=[pl.BlockSpec((tm, D_MODEL), lambda j, i: (i, 0)),
                  pl.BlockSpec((D_MODEL, tn), lambda j, i: (0, j)),
                  pl.BlockSpec((D_MODEL, tn), lambda j, i: (0, j)),
                  pl.BlockSpec((3, tn), lambda j, i: (0, j)),
                  pl.BlockSpec((1, tn), lambda j, i: (0, j))],
        out_specs=pl.BlockSpec((tm, tn), lambda j, i: (i, j)),
        out_shape=jax.ShapeDtypeStruct((m, D_FF), BF16),
        scratch_shapes=[pltpu.VMEM((tm + 8, tn), F32)],
        compiler_params=_params(("arbitrary", "arbitrary")),
        name="ffn_up",
    )(xb, wg, wv, cw, cb)


def _ffn_down_kernel(a_ref, w_ref, x_ref, g_ref, b_ref, o32_ref, *maybe_o16_ref):
    z = DEEPNORM_ALPHA * x_ref[...] + jnp.dot(a_ref[...], w_ref[...], preferred_element_type=F32)
    out = _layer_norm(z, g_ref[...], b_ref[...])
    o32_ref[...] = out
    for o16_ref in maybe_o16_ref:
        o16_ref[...] = out.astype(BF16)


def _ffn_down(a, w, x, g, b, want_bf16):
    m = x.shape[0]
    tm = TM_FFN_DOWN
    row = lambda width: pl.BlockSpec((tm, width), lambda i: (i, 0))
    const = lambda shape: pl.BlockSpec(shape, lambda i: (0, 0))
    out_specs = [row(D_MODEL)]
    out_shape = [jax.ShapeDtypeStruct((m, D_MODEL), F32)]
    if want_bf16:
        out_specs.append(row(D_MODEL))
        out_shape.append(jax.ShapeDtypeStruct((m, D_MODEL), BF16))
    return pl.pallas_call(
        _ffn_down_kernel,
        grid=(m // tm,),
        in_specs=[row(D_FF),
                  pl.BlockSpec((D_FF, D_MODEL), lambda i: (0, 0), pipeline_mode=pl.Buffered(1)),
                  row(D_MODEL), const((1, D_MODEL)), const((1, D_MODEL))],
        out_specs=out_specs,
        out_shape=out_shape,
        compiler_params=_params(("arbitrary",)),
        name="ffn_down_ln",
    )(a, w, x, g, b)


def kernel(x, w_in, fox_f_bias, pool_w, pool_scale, hgrn_lb_logits, hgrn_norm_g, w_out, ln1_g, ln1_b,
           w_gate, w_val, conv_w, conv_b, w_down, ln2_g, ln2_b):
    batch, seq, _ = x.shape
    m = batch * seq
    x32 = x.reshape(m, D_MODEL)
    xb = x32.astype(BF16)
    for l in range(DEPTH):
        w32 = w_in[l, :, :H32_COLS].astype(BF16)
        w16 = w_in[l, :, H32_COLS:H32_COLS + H16_COLS].astype(BF16)
        wff = jnp.pad(w_in[l, :, H32_COLS + H16_COLS:], ((0, 0), (0, FF_PAD - FOX_HEADS))).astype(BF16)
        bias_pad = jnp.pad(fox_f_bias[l], (0, FF_PAD - FOX_HEADS)).reshape(1, FF_PAD)

        h32, h16, ff = _inproj(xb, w32, w16, wff)
        cum, cumt = _fox_cumsum(ff, bias_pad, batch, seq)
        yp = _pool(h32, pool_w[l].astype(BF16), pool_scale[l].reshape(1, POOL_WIDTH), batch, seq)
        yh = _hgrn(h32, hgrn_lb_logits, hgrn_norm_g[l].reshape(1, HGRN_WIDTH), l, batch, seq)
        yf = _fox(h16, cum, cumt, batch, seq)
        x32, xb = _outproj(yp, yh, yf, w_out[l].astype(BF16), x32,
                           ln1_g[l].reshape(1, D_MODEL), ln1_b[l].reshape(1, D_MODEL))
        a = _ffn_up(xb, w_gate[l].astype(BF16), w_val[l].astype(BF16), conv_w[l],
                    conv_b[l].reshape(1, D_FF), seq)
        outs = _ffn_down(a, w_down[l].astype(BF16), x32, ln2_g[l].reshape(1, D_MODEL),
                         ln2_b[l].reshape(1, D_MODEL), want_bf16=l + 1 < DEPTH)
        x32 = outs[0]
        if l + 1 < DEPTH:
            xb = outs[1]
    return x32.reshape(batch, seq, D_MODEL)
```

```python
import functools

import jax
import jax.numpy as jnp
import numpy as np
from jax import lax
from jax.experimental import pallas as pl
from jax.experimental.pallas import tpu as pltpu

F32 = jnp.float32
BF16 = jnp.bfloat16

D_MODEL = 2048
DEPTH = 4
POOL_WIDTH = 512
POOL_WINDOWS = (2, 4, 8, 16)
POOL_GROUP = 128
POOL_HALO = 16
HGRN_WIDTH = 512
HGRN_HEAD_DIM = 128
HGRN_HEADS = 4
FOX_WIDTH = 1024
FOX_HEAD_DIM = 128
FOX_HEADS = 8
D_FF = 5632
DEEPNORM_ALPHA = (2 * DEPTH) ** 0.25
LN_EPS = 1e-5
RMS_EPS = 1e-6
MASK_VALUE = -1e30
EXP_CLAMP = 80.0
LOG2_E = 1.4426950408889634
FOX_Q_SCALE = FOX_HEAD_DIM ** -0.5 * LOG2_E

H32_COLS = POOL_WIDTH + 4 * HGRN_WIDTH
H16_COLS = 3 * FOX_WIDTH
FF_PAD = 128

LANE = 128
VMEM_LIMIT = 56 * 1024 * 1024

TM_INPROJ = 512
TC_CUMSUM = 256
TP_POOL = 256
TS_HGRN = 256
TQ_FOX = 512
TM_OUTPROJ = 256
TM_FFN_UP = 1024
TN_FFN_UP = 512
SUB_FFN_UP = 256
TM_FFN_DOWN = 256

HGRN_BLOCK = 128


def _params(sem):
    return pltpu.CompilerParams(dimension_semantics=sem, vmem_limit_bytes=VMEM_LIMIT)


def _log_sigmoid(x):
    return jnp.minimum(x, 0.0) - jnp.log1p(jnp.exp(-jnp.abs(x)))


def _split3(x):
    a = x.astype(BF16)
    r = x - a.astype(F32)
    b = r.astype(BF16)
    c = (r - b.astype(F32)).astype(BF16)
    return jnp.concatenate([a, b, c], axis=1)


def _sum3(y, n):
    return y[:, 0:n] + y[:, n:2 * n] + y[:, 2 * n:3 * n]


def _split2(x):
    a = x.astype(BF16)
    b = (x - a.astype(F32)).astype(BF16)
    return jnp.concatenate([a, b], axis=1)


def _sum2(y, n):
    return y[:, 0:n] + y[:, n:2 * n]


def _inproj32_kernel(x_ref, w_ref, o_ref):
    o_ref[...] = jnp.dot(x_ref[...], w_ref[...], preferred_element_type=F32)


def _inproj16_kernel(x_ref, w_ref, wff_ref, o_ref, ff_ref):
    x = x_ref[...]
    o_ref[...] = jnp.dot(x, w_ref[...], preferred_element_type=F32).astype(BF16)
    ff_ref[...] = jnp.dot(x, wff_ref[...], preferred_element_type=F32)


def _inproj(xb, w32, w16, wff):
    m = xb.shape[0]
    tm = TM_INPROJ
    h32 = pl.pallas_call(
        _inproj32_kernel,
        grid=(m // tm,),
        in_specs=[pl.BlockSpec((tm, D_MODEL), lambda i: (i, 0)),
                  pl.BlockSpec((D_MODEL, H32_COLS), lambda i: (0, 0))],
        out_specs=pl.BlockSpec((tm, H32_COLS), lambda i: (i, 0)),
        out_shape=jax.ShapeDtypeStruct((m, H32_COLS), F32),
        compiler_params=_params(("arbitrary",)),
        name="inproj32",
    )(xb, w32)
    h16, ff = pl.pallas_call(
        _inproj16_kernel,
        grid=(m // tm,),
        in_specs=[pl.BlockSpec((tm, D_MODEL), lambda i: (i, 0)),
                  pl.BlockSpec((D_MODEL, H16_COLS), lambda i: (0, 0)),
                  pl.BlockSpec((D_MODEL, FF_PAD), lambda i: (0, 0))],
        out_specs=[pl.BlockSpec((tm, H16_COLS), lambda i: (i, 0)),
                   pl.BlockSpec((tm, FF_PAD), lambda i: (i, 0))],
        out_shape=[jax.ShapeDtypeStruct((m, H16_COLS), BF16),
                   jax.ShapeDtypeStruct((m, FF_PAD), F32)],
        compiler_params=_params(("arbitrary",)),
        name="inproj16",
    )(xb, w16, wff)
    return h32, h16, ff


def _fox_aug_constants():
    sel = np.zeros((3 * FF_PAD, 2 * FOX_WIDTH), np.float32)
    ones = np.zeros((1, 2 * FOX_WIDTH), np.float32)
    for h in range(FOX_HEADS):
        for p in range(3):
            sel[p * FF_PAD + h, h * FOX_HEAD_DIM + p] = 1.0
            sel[p * FF_PAD + h, FOX_WIDTH + h * FOX_HEAD_DIM + 3 + p] = -1.0
            ones[0, h * FOX_HEAD_DIM + 3 + p] = 1.0
            ones[0, FOX_WIDTH + h * FOX_HEAD_DIM + p] = 1.0
    return sel, ones


def _cum_kernel(ff_ref, bias_ref, sel_ref, ones_ref, qa_ref, ka_ref, carry_ref):
    tc = ff_ref.shape[0]

    @pl.when(pl.program_id(1) == 0)
    def _():
        carry_ref[...] = jnp.zeros_like(carry_ref)

    lf = _log_sigmoid(ff_ref[...] + bias_ref[...])
    row = lax.broadcasted_iota(jnp.int32, (tc, tc), 0)
    col = lax.broadcasted_iota(jnp.int32, (tc, tc), 1)
    tril = jnp.where(row >= col, 1.0, 0.0).astype(BF16)
    c = _sum3(jnp.dot(tril, _split3(lf), preferred_element_type=F32), LANE) + carry_ref[...]
    carry_ref[...] = c[tc - 1:tc, :]
    aug = jnp.dot(_split3(c * LOG2_E), sel_ref[...], preferred_element_type=F32) + ones_ref[...]
    qa_ref[...] = aug[:, :FOX_WIDTH].astype(BF16)
    ka_ref[...] = aug[:, FOX_WIDTH:].astype(BF16)


def _fox_cumsum(ff, bias_pad, batch, seq):
    tc = TC_CUMSUM
    nt = seq // tc
    sel, ones = _fox_aug_constants()
    row = lambda width: pl.BlockSpec((tc, width), lambda b, i: (b * nt + i, 0))
    const = lambda shape: pl.BlockSpec(shape, lambda b, i: (0, 0))
    return pl.pallas_call(
        _cum_kernel,
        grid=(batch, nt),
        in_specs=[row(FF_PAD), const((1, FF_PAD)), const(sel.shape), const(ones.shape)],
        out_specs=[row(FOX_WIDTH), row(FOX_WIDTH)],
        out_shape=[jax.ShapeDtypeStruct((batch * seq, FOX_WIDTH), BF16),
                   jax.ShapeDtypeStruct((batch * seq, FOX_WIDTH), BF16)],
        scratch_shapes=[pltpu.VMEM((1, FF_PAD), F32)],
        compiler_params=_params(("arbitrary", "arbitrary")),
        name="fox_cumsum",
    )(ff, bias_pad, jnp.asarray(sel, dtype=BF16), jnp.asarray(ones))


def _pool_kernel(u_ref, pw_ref, ps_ref, o_ref, ext_ref):
    tp = u_ref.shape[0]
    i = pl.program_id(1)

    @pl.when(i == 0)
    def _():
        ext_ref[0:POOL_HALO, :] = jnp.zeros((POOL_HALO, POOL_WIDTH), F32)

    u = u_ref[...]
    ext_ref[POOL_HALO:POOL_HALO + tp, :] = u
    pos = (i * tp + lax.broadcasted_iota(jnp.int32, (tp, 1), 0)).astype(F32)
    for gi, w in enumerate(POOL_WINDOWS):
        lanes = slice(gi * POOL_GROUP, (gi + 1) * POOL_GROUP)
        ug = u[:, lanes]
        acc = ug
        for j in range(1, w):
            acc = acc + ext_ref[POOL_HALO - j:POOL_HALO - j + tp, lanes]
        count = jnp.minimum(pos + 1.0, float(w))
        d = (acc / count - ug).astype(BF16)
        y = jnp.dot(d, pw_ref[gi], preferred_element_type=F32) * ps_ref[:, lanes]
        o_ref[:, lanes] = y.astype(BF16)
    ext_ref[0:POOL_HALO, :] = ext_ref[tp:tp + POOL_HALO, :]


def _pool(h32, pool_w, pool_scale, batch, seq):
    tp = TP_POOL
    nt = seq // tp
    return pl.pallas_call(
        _pool_kernel,
        grid=(batch, nt),
        in_specs=[pl.BlockSpec((tp, POOL_WIDTH), lambda b, i: (b * nt + i, 0)),
                  pl.BlockSpec((len(POOL_WINDOWS), POOL_GROUP, POOL_GROUP), lambda b, i: (0, 0, 0)),
                  pl.BlockSpec((1, POOL_WIDTH), lambda b, i: (0, 0))],
        out_specs=pl.BlockSpec((tp, POOL_WIDTH), lambda b, i: (b * nt + i, 0)),
        out_shape=jax.ShapeDtypeStruct((batch * seq, POOL_WIDTH), BF16),
        scratch_shapes=[pltpu.VMEM((tp + POOL_HALO, POOL_WIDTH), F32)],
        compiler_params=_params(("arbitrary", "arbitrary")),
        name="pool_mixer",
    )(h32, pool_w, pool_scale)


def _hgrn_levels(c):
    return tuple(c >> k for k in range(1, c.bit_length()))


def _hgrn_level_matrix(c):
    t = np.arange(c)[:, None]
    u = np.arange(c)[None, :]
    mats = [(u <= t)]
    for m in _hgrn_levels(c):
        p = t % (2 * m)
        r = t - p + m - 1
        upper = p >= m
        mats.append(np.where(upper, (u > r) & (u <= t), (u > t) & (u <= r)))
    return np.concatenate(mats, axis=0).astype(np.float32)


def _hgrn_kernel(q_ref, z_ref, v_ref, g_ref, lbl_ref, ng_ref, gm_ref, o_ref, st_ref, *, layer):
    c = q_ref.shape[0]
    d = HGRN_HEAD_DIM
    blk = min(c, HGRN_BLOCK)
    nblk = c // blk
    levels = _hgrn_levels(c)

    @pl.when(pl.program_id(1) == 0)
    def _():
        st_ref[...] = jnp.zeros_like(st_ref)

    lbl = lbl_ref[...]
    e = jnp.exp(lbl - jnp.max(lbl, axis=0, keepdims=True))
    lbw = e / jnp.sum(e, axis=0, keepdims=True)
    lb_all = jnp.zeros((1, HGRN_WIDTH), F32)
    for li in range(1, layer + 1):
        lb_all = lb_all + lbw[li:li + 1, :]

    gm = gm_ref[...]
    row_d = lax.broadcasted_iota(jnp.int32, (c, d), 0)
    row_b = lax.broadcasted_iota(jnp.int32, (blk, blk), 0)
    col_b = lax.broadcasted_iota(jnp.int32, (blk, blk), 1)
    same = row_b ^ col_b
    dn_t = (((1,), (1,)), ((), ()))
    dn_a = (((0,), (0,)), ((), ()))
    rows = lambda bi: slice(bi * blk, (bi + 1) * blk)

    for h in range(HGRN_HEADS):
        lanes = slice(h * d, (h + 1) * d)
        lb = lb_all[:, lanes]
        q = q_ref[:, lanes]
        z = z_ref[:, lanes]
        v = v_ref[:, lanes].astype(BF16)

        log_f = _log_sigmoid(z) + jnp.log1p(lb * jnp.exp(jnp.minimum(-z, EXP_CLAMP)))
        log_f = jnp.minimum(log_f, 0.0)
        kk = (1.0 - lb) / (1.0 + jnp.exp(z))

        ex = _sum2(jnp.dot(gm, _split2(log_f), preferred_element_type=F32), d)
        b = ex[0:c]
        b_last = b[c - 1:c, :]
        st = st_ref[h]

        qi = (q * jnp.exp(b)).astype(BF16)
        o_inter = lax.dot_general(qi, st.astype(BF16), dn_t, preferred_element_type=F32)
        kd = (kk * jnp.exp(b_last - b)).astype(BF16)
        st_ref[h] = st * jnp.exp(b_last) + lax.dot_general(v, kd, dn_a, preferred_element_type=F32)

        diag = [None] * nblk
        off = {}
        for li, m in enumerate(levels):
            upper = (row_d & m) != 0
            x = (jnp.where(upper, q, kk) * jnp.exp(ex[(li + 1) * c:(li + 2) * c])).astype(BF16)
            for bi in range(nblk):
                if m < blk:
                    a = lax.dot_general(x[rows(bi)], x[rows(bi)], dn_t, preferred_element_type=F32)
                    diag[bi] = a if diag[bi] is None else jnp.where(same < 2 * m, a, diag[bi])
                elif (bi * blk) & m:
                    for bj in range(nblk):
                        if not (bj * blk) & m and (bi * blk) // (2 * m) == (bj * blk) // (2 * m):
                            off[bi, bj] = lax.dot_general(x[rows(bi)], x[rows(bj)], dn_t,
                                                          preferred_element_type=F32)
        qb = q.astype(BF16)
        kb = kk.astype(BF16)
        g = g_ref[:, lanes]
        gate = ng_ref[:, lanes] * (g / (1.0 + jnp.exp(-g)))
        for bi in range(nblk):
            a = lax.dot_general(qb[rows(bi)], kb[rows(bi)], dn_t, preferred_element_type=F32)
            att = jnp.where(same == 0, a, diag[bi])
            att = jnp.where(row_b >= col_b, att, 0.0)
            o = o_inter[rows(bi)] + jnp.dot(att.astype(BF16), v[rows(bi)], preferred_element_type=F32)
            for bj in range(bi):
                o = o + jnp.dot(off[bi, bj].astype(BF16), v[rows(bj)], preferred_element_type=F32)
            o = o * lax.rsqrt(jnp.mean(o * o, axis=-1, keepdims=True) + RMS_EPS)
            o_ref[rows(bi), lanes] = (o * gate[rows(bi)]).astype(BF16)


def _hgrn(h32, lb_logits, norm_g, layer, batch, seq):
    ts = TS_HGRN
    nt = seq // ts
    gm = jnp.asarray(_hgrn_level_matrix(ts), dtype=BF16)

    def col(k):
        return pl.BlockSpec((ts, HGRN_WIDTH), lambda b, i: (b * nt + i, k))

    return pl.pallas_call(
        functools.partial(_hgrn_kernel, layer=layer),
        grid=(batch, nt),
        in_specs=[col(1), col(2), col(3), col(4),
                  pl.BlockSpec((DEPTH, HGRN_WIDTH), lambda b, i: (0, 0)),
                  pl.BlockSpec((1, HGRN_WIDTH), lambda b, i: (0, 0)),
                  pl.BlockSpec(gm.shape, lambda b, i: (0, 0))],
        out_specs=pl.BlockSpec((ts, HGRN_WIDTH), lambda b, i: (b * nt + i, 0)),
        out_shape=jax.ShapeDtypeStruct((batch * seq, HGRN_WIDTH), BF16),
        scratch_shapes=[pltpu.VMEM((HGRN_HEADS, HGRN_HEAD_DIM, HGRN_HEAD_DIM), F32)],
        compiler_params=_params(("arbitrary", "arbitrary")),
        name="hgrn_mixer",
    )(h32, h32, h32, h32, lb_logits, norm_g, gm)


def _fox_kernel(q_ref, k_ref, v_ref, qa_ref, ka_ref, o_ref, m_ref, acc_ref):
    tq = q_ref.shape[0]
    tk = k_ref.shape[0]
    i = pl.program_id(1)
    j = pl.program_id(2)
    dh = FOX_HEAD_DIM
    dn_t = (((1,), (1,)), ((), ()))

    @pl.when(j == 0)
    def _():
        m_ref[...] = jnp.full(m_ref.shape, MASK_VALUE, F32)
        acc_ref[...] = jnp.zeros_like(acc_ref)

    def kv_step(on_diagonal):
        if on_diagonal:
            causal = (lax.broadcasted_iota(jnp.int32, (tq, tk), 0)
                      >= lax.broadcasted_iota(jnp.int32, (tq, tk), 1))
        ones = jnp.ones((tk, dh), BF16)
        scores, probs, alphas = [], [], []
        for h in range(FOX_HEADS):
            lanes = slice(h * dh, (h + 1) * dh)
            qc = jnp.concatenate([q_ref[:, lanes], qa_ref[:, lanes]], axis=1)
            kc = jnp.concatenate([k_ref[:, lanes], ka_ref[:, lanes]], axis=1)
            scores.append(lax.dot_general(qc, kc, dn_t, preferred_element_type=F32))
        for h in range(FOX_HEADS):
            s = scores[h]
            if on_diagonal:
                s = jnp.where(causal, s, MASK_VALUE)
            m_prev = m_ref[h]
            m_new = jnp.maximum(m_prev, jnp.broadcast_to(jnp.max(s, axis=-1, keepdims=True), m_prev.shape))
            alphas.append(jnp.exp2(m_prev - m_new))
            probs.append(jnp.exp2(s - jnp.concatenate([m_new] * (tk // dh), axis=1)).astype(BF16))
            m_ref[h] = m_new
        for h in range(FOX_HEADS):
            lanes = slice(h * dh, (h + 1) * dh)
            pv = jnp.dot(probs[h], jnp.concatenate([v_ref[:, lanes], ones], axis=1),
                         preferred_element_type=F32)
            acc_ref[h] = jnp.concatenate([alphas[h], alphas[h]], axis=1) * acc_ref[h] + pv

    @pl.when(j < i)
    def _():
        kv_step(False)

    @pl.when(j == i)
    def _():
        kv_step(True)
        for h in range(FOX_HEADS):
            lanes = slice(h * dh, (h + 1) * dh)
            acc = acc_ref[h]
            o_ref[:, lanes] = (acc[:, :dh] / acc[:, dh:]).astype(BF16)


def _fox(h16, qa, ka, batch, seq):
    tq = TQ_FOX
    nq = seq // tq
    q_row = lambda col: pl.BlockSpec((tq, FOX_WIDTH), lambda b, i, j: (b * nq + i, col))
    kv_row = lambda col: pl.BlockSpec((tq, FOX_WIDTH), lambda b, i, j: (b * nq + jnp.minimum(j, i), col))
    return pl.pallas_call(
        _fox_kernel,
        grid=(batch, nq, nq),
        in_specs=[q_row(0), kv_row(1), kv_row(2), q_row(0), kv_row(0)],
        out_specs=q_row(0),
        out_shape=jax.ShapeDtypeStruct((batch * seq, FOX_WIDTH), BF16),
        scratch_shapes=[pltpu.VMEM((FOX_HEADS, tq, FOX_HEAD_DIM), F32),
                        pltpu.VMEM((FOX_HEADS, tq, 2 * FOX_HEAD_DIM), F32)],
        compiler_params=_params(("arbitrary", "arbitrary", "arbitrary")),
        name="fox_attention",
    )(h16, h16, h16, qa, ka)


def _layer_norm(z, g, b):
    mu = jnp.mean(z, axis=-1, keepdims=True)
    zc = z - mu
    var = jnp.mean(zc * zc, axis=-1, keepdims=True)
    return zc * lax.rsqrt(var + LN_EPS) * g + b


def _outproj_kernel(yp_ref, yh_ref, yf_ref, w_ref, x_ref, g_ref, b_ref, o32_ref, o16_ref):
    y = jnp.concatenate([yp_ref[...], yh_ref[...], yf_ref[...]], axis=1)
    z = DEEPNORM_ALPHA * x_ref[...] + jnp.dot(y, w_ref[...], preferred_element_type=F32)
    out = _layer_norm(z, g_ref[...], b_ref[...])
    o32_ref[...] = out
    o16_ref[...] = out.astype(BF16)


def _outproj(yp, yh, yf, w, x, g, b):
    m = x.shape[0]
    tm = TM_OUTPROJ
    row = lambda width: pl.BlockSpec((tm, width), lambda i: (i, 0))
    const = lambda shape: pl.BlockSpec(shape, lambda i: (0, 0))
    return pl.pallas_call(
        _outproj_kernel,
        grid=(m // tm,),
        in_specs=[row(POOL_WIDTH), row(HGRN_WIDTH), row(FOX_WIDTH), const((D_MODEL, D_MODEL)),
                  row(D_MODEL), const((1, D_MODEL)), const((1, D_MODEL))],
        out_specs=[row(D_MODEL), row(D_MODEL)],
        out_shape=[jax.ShapeDtypeStruct((m, D_MODEL), F32), jax.ShapeDtypeStruct((m, D_MODEL), BF16)],
        compiler_params=_params(("arbitrary",)),
        name="outproj_ln",
    )(yp, yh, yf, w, x, g, b)


def _ffn_up_kernel(x_ref, wg_ref, wv_ref, cw_ref, cb_ref, o_ref, gbuf_ref, *, tiles_per_seq):
    tm = x_ref.shape[0]
    sub = min(tm, SUB_FFN_UP)
    i = pl.program_id(1)

    @pl.when(i % tiles_per_seq == 0)
    def _():
        gbuf_ref[0:8, :] = jnp.zeros((8, gbuf_ref.shape[1]), F32)

    for r0 in range(0, tm, sub):
        x = x_ref[r0:r0 + sub, :]
        gate = jnp.dot(x, wg_ref[...], preferred_element_type=F32)
        val = jnp.dot(x, wv_ref[...], preferred_element_type=F32)
        gbuf_ref[8 + r0:8 + r0 + sub, :] = gate
        conv = (cw_ref[2:3, :] * gate + cw_ref[1:2, :] * gbuf_ref[7 + r0:7 + r0 + sub, :]
                + cw_ref[0:1, :] * gbuf_ref[6 + r0:6 + r0 + sub, :] + cb_ref[...])
        o_ref[r0:r0 + sub, :] = (conv / (1.0 + jnp.exp(-conv)) * val).astype(BF16)
    gbuf_ref[0:8, :] = gbuf_ref[tm:tm + 8, :]


def _ffn_up(xb, wg, wv, cw, cb, seq):
    m = xb.shape[0]
    tm, tn = TM_FFN_UP, TN_FFN_UP
    return pl.pallas_call(
        functools.partial(_ffn_up_kernel, tiles_per_seq=seq // tm),
        grid=(D_FF // tn, m // tm),
        in_specs=[pl.BlockSpec((tm, D_MODEL), lambda j, i: (i, 0)),
                  pl.BlockSpec((D_MODEL, tn), lambda j, i: (0, j)),
                  pl.BlockSpec((D_MODEL, tn), lambda j, i: (0, j)),
                  pl.BlockSpec((3, tn), lambda j, i: (0, j)),
                  pl.BlockSpec((1, tn), lambda j, i: (0, j))],
        out_specs=pl.BlockSpec((tm, tn), lambda j, i: (i, j)),
        out_shape=jax.ShapeDtypeStruct((m, D_FF), BF16),
        scratch_shapes=[pltpu.VMEM((tm + 8, tn), F32)],
        compiler_params=_params(("arbitrary", "arbitrary")),
        name="ffn_up",
    )(xb, wg, wv, cw, cb)


def _ffn_down_kernel(a_ref, w_ref, x_ref, g_ref, b_ref, o32_ref, *maybe_o16_ref):
    z = DEEPNORM_ALPHA * x_ref[...] + jnp.dot(a_ref[...], w_ref[...], preferred_element_type=F32)
    out = _layer_norm(z, g_ref[...], b_ref[...])
    o32_ref[...] = out
    for o16_ref in maybe_o16_ref:
        o16_ref[...] = out.astype(BF16)


def _ffn_down(a, w, x, g, b, want_bf16):
    m = x.shape[0]
    tm = TM_FFN_DOWN
    row = lambda width: pl.BlockSpec((tm, width), lambda i: (i, 0))
    const = lambda shape: pl.BlockSpec(shape, lambda i: (0, 0))
    out_specs = [row(D_MODEL)]
    out_shape = [jax.ShapeDtypeStruct((m, D_MODEL), F32)]
    if want_bf16:
        out_specs.append(row(D_MODEL))
        out_shape.append(jax.ShapeDtypeStruct((m, D_MODEL), BF16))
    return pl.pallas_call(
        _ffn_down_kernel,
        grid=(m // tm,),
        in_specs=[row(D_FF),
                  pl.BlockSpec((D_FF, D_MODEL), lambda i: (0, 0), pipeline_mode=pl.Buffered(1)),
                  row(D_MODEL), const((1, D_MODEL)), const((1, D_MODEL))],
        out_specs=out_specs,
        out_shape=out_shape,
        compiler_params=_params(("arbitrary",)),
        name="ffn_down_ln",
    )(a, w, x, g, b)


def kernel(x, w_in, fox_f_bias, pool_w, pool_scale, hgrn_lb_logits, hgrn_norm_g, w_out, ln1_g, ln1_b,
           w_gate, w_val, conv_w, conv_b, w_down, ln2_g, ln2_b):
    batch, seq, _ = x.shape
    m = batch * seq
    x32 = x.reshape(m, D_MODEL)
    xb = x32.astype(BF16)
    for l in range(DEPTH):
        w32 = w_in[l, :, :H32_COLS].astype(BF16)
        w16 = jnp.concatenate([w_in[l, :, H32_COLS:H32_COLS + FOX_WIDTH] * FOX_Q_SCALE,
                               w_in[l, :, H32_COLS + FOX_WIDTH:H32_COLS + H16_COLS]], axis=1).astype(BF16)
        wff = jnp.pad(w_in[l, :, H32_COLS + H16_COLS:], ((0, 0), (0, FF_PAD - FOX_HEADS))).astype(BF16)
        bias_pad = jnp.pad(fox_f_bias[l], (0, FF_PAD - FOX_HEADS)).reshape(1, FF_PAD)

        h32, h16, ff = _inproj(xb, w32, w16, wff)
        qa, ka = _fox_cumsum(ff, bias_pad, batch, seq)
        yp = _pool(h32, pool_w[l].astype(BF16), pool_scale[l].reshape(1, POOL_WIDTH), batch, seq)
        yh = _hgrn(h32, hgrn_lb_logits, hgrn_norm_g[l].reshape(1, HGRN_WIDTH), l, batch, seq)
        yf = _fox(h16, qa, ka, batch, seq)
        x32, xb = _outproj(yp, yh, yf, w_out[l].astype(BF16), x32,
                           ln1_g[l].reshape(1, D_MODEL), ln1_b[l].reshape(1, D_MODEL))
        a = _ffn_up(xb, w_gate[l].astype(BF16), w_val[l].astype(BF16), conv_w[l],
                    conv_b[l].reshape(1, D_FF), seq)
        outs = _ffn_down(a, w_down[l].astype(BF16), x32, ln2_g[l].reshape(1, D_MODEL),
                         ln2_b[l].reshape(1, D_MODEL), want_bf16=l + 1 < DEPTH)
        x32 = outs[0]
        if l + 1 < DEPTH:
            xb = outs[1]
    return x32.reshape(batch, seq, D_MODEL)
```

```python
import functools

import jax
import jax.numpy as jnp
import numpy as np
from jax import lax
from jax.experimental import pallas as pl
from jax.experimental.pallas import tpu as pltpu

F32 = jnp.float32
BF16 = jnp.bfloat16

D_MODEL = 2048
DEPTH = 4
POOL_WIDTH = 512
POOL_WINDOWS = (2, 4, 8, 16)
POOL_GROUP = 128
POOL_HALO = 16
HGRN_WIDTH = 512
HGRN_HEAD_DIM = 128
HGRN_HEADS = 4
FOX_WIDTH = 1024
FOX_HEAD_DIM = 128
FOX_HEADS = 8
D_FF = 5632
DEEPNORM_ALPHA = (2 * DEPTH) ** 0.25
LN_EPS = 1e-5
RMS_EPS = 1e-6
MASK_VALUE = -1e30
EXP_CLAMP = 80.0
LOG2_E = 1.4426950408889634
FOX_Q_SCALE = FOX_HEAD_DIM ** -0.5 * LOG2_E

H32_COLS = POOL_WIDTH + 4 * HGRN_WIDTH
H16_COLS = 3 * FOX_WIDTH
FF_PAD = 128
FOX_AUG_LANES = 16

DOT_NT = (((1,), (1,)), ((), ()))
LANE = 128
SUBLANES = 8
VMEM_LIMIT = 56 * 1024 * 1024

TM_INPROJ = 512
TS_HGRN = 256
TQ_FOX = 512
TM_OUTPROJ = 256
TM_FFN_UP = 1024
TN_FFN_UP = 512
SUB_FFN_UP = 256
TM_FFN_DOWN = 256
SUB_LN = 128

HGRN_BLOCK = 128


def _params(sem):
    return pltpu.CompilerParams(dimension_semantics=sem, vmem_limit_bytes=VMEM_LIMIT)


def _log_sigmoid(x):
    return jnp.minimum(x, 0.0) - jnp.log1p(jnp.exp(-jnp.abs(x)))


def _split3(x):
    a = x.astype(BF16)
    r = x - a.astype(F32)
    b = r.astype(BF16)
    c = (r - b.astype(F32)).astype(BF16)
    return jnp.concatenate([a, b, c], axis=1)


def _sum3(y, n):
    return y[:, 0:n] + y[:, n:2 * n] + y[:, 2 * n:3 * n]


def _split2(x):
    a = x.astype(BF16)
    b = (x - a.astype(F32)).astype(BF16)
    return jnp.concatenate([a, b], axis=1)


def _sum2(y, n):
    return y[:, 0:n] + y[:, n:2 * n]


def _inproj32_kernel(x_ref, w_ref, pw_ref, ps_ref, h_ref, yp_ref, ext_ref, *, tiles_per_seq):
    tm = x_ref.shape[0]
    i = pl.program_id(0)

    @pl.when(i % tiles_per_seq == 0)
    def _():
        ext_ref[0:POOL_HALO, :] = jnp.zeros((POOL_HALO, POOL_WIDTH), F32)

    x = x_ref[...].astype(BF16)
    u = lax.dot_general(x, w_ref[0:POOL_WIDTH, :], DOT_NT, preferred_element_type=F32)
    h_ref[...] = lax.dot_general(x, w_ref[POOL_WIDTH:, :], DOT_NT, preferred_element_type=F32)

    ext_ref[POOL_HALO:POOL_HALO + tm, :] = u
    pos = ((i % tiles_per_seq) * tm + lax.broadcasted_iota(jnp.int32, (tm, 1), 0)).astype(F32)
    for gi, w in enumerate(POOL_WINDOWS):
        lanes = slice(gi * POOL_GROUP, (gi + 1) * POOL_GROUP)
        ug = u[:, lanes]
        acc = ug
        for j in range(1, w):
            acc = acc + ext_ref[POOL_HALO - j:POOL_HALO - j + tm, lanes]
        count = jnp.minimum(pos + 1.0, float(w))
        d = (acc / count - ug).astype(BF16)
        y = jnp.dot(d, pw_ref[gi], preferred_element_type=F32) * ps_ref[:, lanes]
        yp_ref[:, lanes] = y.astype(BF16)
    ext_ref[0:POOL_HALO, :] = ext_ref[tm:tm + POOL_HALO, :]


def _inproj16_kernel(x_ref, w_ref, wff_ref, bias_ref, tril_ref, sel_ref, ones_ref,
                     o_ref, qa_ref, ka_ref, carry_ref, *, tiles_per_seq):
    tm = x_ref.shape[0]

    @pl.when(pl.program_id(0) % tiles_per_seq == 0)
    def _():
        carry_ref[...] = jnp.zeros_like(carry_ref)

    x = x_ref[...].astype(BF16)
    ff = lax.dot_general(x, wff_ref[...], DOT_NT, preferred_element_type=F32)
    lf = _log_sigmoid(ff + bias_ref[...])
    c = _sum3(jnp.dot(tril_ref[...], _split3(lf), preferred_element_type=F32), LANE) + carry_ref[...]
    carry_ref[...] = c[tm - 1:tm, :]
    aug = jnp.dot(_split3(c * LOG2_E), sel_ref[...], preferred_element_type=F32) + ones_ref[...]
    qa_ref[...] = aug[:, :LANE].astype(BF16)
    ka_ref[...] = aug[:, LANE:].astype(BF16)
    o_ref[...] = lax.dot_general(x, w_ref[...], DOT_NT, preferred_element_type=F32).astype(BF16)


def _inproj(xb, w32, w16, wff, bias_pad, pool_w, pool_scale, seq):
    m = xb.shape[0]
    tm = TM_INPROJ
    row = lambda width: pl.BlockSpec((tm, width), lambda i: (i, 0))
    const = lambda shape: pl.BlockSpec(shape, lambda i: (0,) * len(shape))
    h32, yp = pl.pallas_call(
        functools.partial(_inproj32_kernel, tiles_per_seq=seq // tm),
        grid=(m // tm,),
        in_specs=[row(D_MODEL), const((H32_COLS, D_MODEL)),
                  const((len(POOL_WINDOWS), POOL_GROUP, POOL_GROUP)), const((1, POOL_WIDTH))],
        out_specs=[row(4 * HGRN_WIDTH), row(POOL_WIDTH)],
        out_shape=[jax.ShapeDtypeStruct((m, 4 * HGRN_WIDTH), F32),
                   jax.ShapeDtypeStruct((m, POOL_WIDTH), BF16)],
        scratch_shapes=[pltpu.VMEM((tm + POOL_HALO, POOL_WIDTH), F32)],
        compiler_params=_params(("arbitrary",)),
        name="inproj32_pool",
    )(xb, w32, pool_w, pool_scale)
    sel, ones = _fox_aug_constants()
    tril = np.tril(np.ones((tm, tm), np.float32))
    h16, qa, ka = pl.pallas_call(
        functools.partial(_inproj16_kernel, tiles_per_seq=seq // tm),
        grid=(m // tm,),
        in_specs=[row(D_MODEL), const((H16_COLS, D_MODEL)), const((FF_PAD, D_MODEL)), const((1, FF_PAD)),
                  const(tril.shape), const(sel.shape), const(ones.shape)],
        out_specs=[row(H16_COLS), row(LANE), row(LANE)],
        out_shape=[jax.ShapeDtypeStruct((m, H16_COLS), BF16),
                   jax.ShapeDtypeStruct((m, LANE), BF16),
                   jax.ShapeDtypeStruct((m, LANE), BF16)],
        scratch_shapes=[pltpu.VMEM((1, FF_PAD), F32)],
        compiler_params=_params(("arbitrary",)),
        name="inproj16_cumsum",
    )(xb, w16, wff, bias_pad, jnp.asarray(tril, dtype=BF16), jnp.asarray(sel, dtype=BF16), jnp.asarray(ones))
    return h32, yp, h16, qa, ka


def _fox_aug_constants():
    sel = np.zeros((3 * FF_PAD, 2 * LANE), np.float32)
    ones = np.zeros((1, 2 * LANE), np.float32)
    for h in range(FOX_HEADS):
        for p in range(3):
            sel[p * FF_PAD + h, h * FOX_AUG_LANES + p] = 1.0
            sel[p * FF_PAD + h, LANE + h * FOX_AUG_LANES + 3 + p] = -1.0
            ones[0, h * FOX_AUG_LANES + 3 + p] = 1.0
            ones[0, LANE + h * FOX_AUG_LANES + p] = 1.0
    return sel, ones


def _hgrn_levels(c):
    return tuple(c >> k for k in range(1, c.bit_length()))


HGRN_MATRIX_LEVELS = (4, 2)


def _hgrn_level_matrix(c):
    t = np.arange(c)[:, None]
    u = np.arange(c)[None, :]
    mats = [(u <= t)]
    for m in HGRN_MATRIX_LEVELS:
        p = t % (2 * m)
        r = t - p + m - 1
        upper = p >= m
        mats.append(np.where(upper, (u > r) & (u <= t), (u > t) & (u <= r)))
    return np.concatenate(mats, axis=0).astype(np.float32)


def _hgrn_kernel(q_ref, z_ref, v_ref, g_ref, lbl_ref, ng_ref, gm_ref, o_ref, st_ref, *, layer):
    c = q_ref.shape[0]
    d = HGRN_HEAD_DIM
    blk = min(c, HGRN_BLOCK)
    nblk = c // blk
    levels = _hgrn_levels(c)

    @pl.when(pl.program_id(1) == 0)
    def _():
        st_ref[...] = jnp.zeros_like(st_ref)

    lbl = lbl_ref[...]
    e = jnp.exp(lbl - jnp.max(lbl, axis=0, keepdims=True))
    lbw = e / jnp.sum(e, axis=0, keepdims=True)
    lb_all = jnp.zeros((1, HGRN_WIDTH), F32)
    for li in range(1, layer + 1):
        lb_all = lb_all + lbw[li:li + 1, :]

    gm = gm_ref[...]
    row_d = lax.broadcasted_iota(jnp.int32, (c, d), 0)
    row_b = lax.broadcasted_iota(jnp.int32, (blk, blk), 0)
    col_b = lax.broadcasted_iota(jnp.int32, (blk, blk), 1)
    same = row_b ^ col_b
    dn_t = (((1,), (1,)), ((), ()))
    dn_a = (((0,), (0,)), ((), ()))
    rows = lambda bi: slice(bi * blk, (bi + 1) * blk)

    heads = range(HGRN_HEADS)
    lane = lambda h: slice(h * d, (h + 1) * d)
    q = [q_ref[:, lane(h)] for h in heads]
    v = [v_ref[:, lane(h)].astype(BF16) for h in heads]

    kk, lf2 = [], []
    for h in heads:
        z = z_ref[:, lane(h)]
        log_f = jnp.minimum(z, 0.0) - jnp.log(1.0 + jnp.exp(-jnp.abs(z)))
        sig_neg = 1.0 / (1.0 + jnp.exp(z))
        if layer > 0:
            lb = lb_all[:, lane(h)]
            log_f = log_f + jnp.log(1.0 + lb * jnp.exp(jnp.minimum(-z, EXP_CLAMP)))
            sig_neg = (1.0 - lb) * sig_neg
        kk.append(sig_neg)
        lf2.append(jnp.minimum(log_f, 0.0) * LOG2_E)

    ex = [_sum2(jnp.dot(gm, _split2(lf2[h]), preferred_element_type=F32), d) for h in heads]

    o_inter = []
    for h in heads:
        b = ex[h][0:c]
        b_last = b[c - 1:c, :]
        st = st_ref[h]
        qi = (q[h] * jnp.exp2(b)).astype(BF16)
        o_inter.append(lax.dot_general(qi, st.astype(BF16), dn_t, preferred_element_type=F32))
        kd = (kk[h] * jnp.exp2(b_last - b)).astype(BF16)
        st_ref[h] = st * jnp.exp2(b_last) + lax.dot_general(v[h], kd, dn_a, preferred_element_type=F32)

    diag = [[None] * nblk for _ in heads]
    off = [{} for _ in heads]
    for m in levels:
        upper = (row_d & m) != 0
        for h in heads:
            b = ex[h][0:c]
            if m >= SUBLANES:
                parts = []
                for r0 in range(0, c, 2 * m):
                    lo, hi = slice(r0, r0 + m), slice(r0 + m, r0 + 2 * m)
                    b_ref_row = b[r0 + m - 1:r0 + m, :]
                    parts.append(kk[h][lo] * jnp.exp2(b_ref_row - b[lo]))
                    parts.append(q[h][hi] * jnp.exp2(b[hi] - b_ref_row))
                x = jnp.concatenate(parts, axis=0).astype(BF16)
            elif m == 1:
                x = jnp.where(upper, q[h] * jnp.exp2(lf2[h]), kk[h]).astype(BF16)
            else:
                k = 1 + HGRN_MATRIX_LEVELS.index(m)
                x = (jnp.where(upper, q[h], kk[h]) * jnp.exp2(ex[h][k * c:(k + 1) * c])).astype(BF16)
            for bi in range(nblk):
                if m < blk:
                    a = lax.dot_general(x[rows(bi)], x[rows(bi)], dn_t, preferred_element_type=F32)
                    diag[h][bi] = a if diag[h][bi] is None else jnp.where(same < 2 * m, a, diag[h][bi])
                elif (bi * blk) & m:
                    for bj in range(nblk):
                        if not (bj * blk) & m and (bi * blk) // (2 * m) == (bj * blk) // (2 * m):
                            off[h][bi, bj] = lax.dot_general(x[rows(bi)], x[rows(bj)], dn_t,
                                                             preferred_element_type=F32)

    for h in heads:
        qb = q[h].astype(BF16)
        kb = kk[h].astype(BF16)
        g = g_ref[:, lane(h)]
        gate = ng_ref[:, lane(h)] * (g / (1.0 + jnp.exp(-g)))
        for bi in range(nblk):
            a = lax.dot_general(qb[rows(bi)], kb[rows(bi)], dn_t, preferred_element_type=F32)
            att = jnp.where(same == 0, a, diag[h][bi])
            att = jnp.where(row_b >= col_b, att, 0.0)
            o = o_inter[h][rows(bi)] + jnp.dot(att.astype(BF16), v[h][rows(bi)], preferred_element_type=F32)
            for bj in range(bi):
                o = o + jnp.dot(off[h][bi, bj].astype(BF16), v[h][rows(bj)], preferred_element_type=F32)
            o = o * lax.rsqrt(jnp.mean(o * o, axis=-1, keepdims=True) + RMS_EPS)
            o_ref[rows(bi), lane(h)] = (o * gate[rows(bi)]).astype(BF16)


def _hgrn(h32, lb_logits, norm_g, layer, batch, seq):
    ts = TS_HGRN
    nt = seq // ts
    gm = jnp.asarray(_hgrn_level_matrix(ts), dtype=BF16)

    def col(k):
        return pl.BlockSpec((ts, HGRN_WIDTH), lambda b, i: (b * nt + i, k))

    return pl.pallas_call(
        functools.partial(_hgrn_kernel, layer=layer),
        grid=(batch, nt),
        in_specs=[col(0), col(1), col(2), col(3),
                  pl.BlockSpec((DEPTH, HGRN_WIDTH), lambda b, i: (0, 0)),
                  pl.BlockSpec((1, HGRN_WIDTH), lambda b, i: (0, 0)),
                  pl.BlockSpec(gm.shape, lambda b, i: (0, 0))],
        out_specs=pl.BlockSpec((ts, HGRN_WIDTH), lambda b, i: (b * nt + i, 0)),
        out_shape=jax.ShapeDtypeStruct((batch * seq, HGRN_WIDTH), BF16),
        scratch_shapes=[pltpu.VMEM((HGRN_HEADS, HGRN_HEAD_DIM, HGRN_HEAD_DIM), F32)],
        compiler_params=_params(("arbitrary", "arbitrary")),
        name="hgrn_mixer",
    )(h32, h32, h32, h32, lb_logits, norm_g, gm)


def _fox_kernel(q_ref, k_ref, v_ref, qa_ref, ka_ref, o_ref, m_ref, acc_ref):
    tq = q_ref.shape[0]
    tk = k_ref.shape[0]
    i = pl.program_id(1)
    j = pl.program_id(2)
    dh = FOX_HEAD_DIM
    dn_t = (((1,), (1,)), ((), ()))

    @pl.when(j == 0)
    def _():
        m_ref[...] = jnp.full(m_ref.shape, MASK_VALUE, F32)
        acc_ref[...] = jnp.zeros_like(acc_ref)

    def kv_step(on_diagonal):
        if on_diagonal:
            causal = (lax.broadcasted_iota(jnp.int32, (tq, tk), 0)
                      >= lax.broadcasted_iota(jnp.int32, (tq, tk), 1))
        ones = jnp.ones((tk, dh), BF16)
        qa = qa_ref[...]
        ka = ka_ref[...]
        aug_head = lax.broadcasted_iota(jnp.int32, ka.shape, 1) // FOX_AUG_LANES
        scores, probs, alphas = [], [], []
        for h in range(FOX_HEADS):
            lanes = slice(h * dh, (h + 1) * dh)
            qc = jnp.concatenate([q_ref[:, lanes], qa], axis=1)
            kc = jnp.concatenate([k_ref[:, lanes], jnp.where(aug_head == h, ka, jnp.zeros_like(ka))], axis=1)
            scores.append(lax.dot_general(qc, kc, dn_t, preferred_element_type=F32))
        for h in range(FOX_HEADS):
            s = scores[h]
            if on_diagonal:
                s = jnp.where(causal, s, MASK_VALUE)
            m_prev = m_ref[h]
            m_new = jnp.maximum(m_prev, jnp.broadcast_to(jnp.max(s, axis=-1, keepdims=True), m_prev.shape))
            alphas.append(jnp.exp2(m_prev - m_new))
            probs.append(jnp.exp2(s - jnp.concatenate([m_new] * (tk // dh), axis=1)).astype(BF16))
            m_ref[h] = m_new
        for h in range(FOX_HEADS):
            lanes = slice(h * dh, (h + 1) * dh)
            pv = jnp.dot(probs[h], jnp.concatenate([v_ref[:, lanes], ones], axis=1),
                         preferred_element_type=F32)
            acc_ref[h] = jnp.concatenate([alphas[h], alphas[h]], axis=1) * acc_ref[h] + pv

    @pl.when(j < i)
    def _():
        kv_step(False)

    @pl.when(j == i)
    def _():
        kv_step(True)
        for h in range(FOX_HEADS):
            lanes = slice(h * dh, (h + 1) * dh)
            acc = acc_ref[h]
            o_ref[:, lanes] = (acc[:, :dh] / acc[:, dh:]).astype(BF16)


def _fox(h16, qa, ka, batch, seq):
    tq = TQ_FOX
    nq = seq // tq
    q_row = lambda col: pl.BlockSpec((tq, FOX_WIDTH), lambda b, i, j: (b * nq + i, col))
    kv_row = lambda col: pl.BlockSpec((tq, FOX_WIDTH), lambda b, i, j: (b * nq + jnp.minimum(j, i), col))
    q_aug = pl.BlockSpec((tq, LANE), lambda b, i, j: (b * nq + i, 0))
    kv_aug = pl.BlockSpec((tq, LANE), lambda b, i, j: (b * nq + jnp.minimum(j, i), 0))
    return pl.pallas_call(
        _fox_kernel,
        grid=(batch, nq, nq),
        in_specs=[q_row(0), kv_row(1), kv_row(2), q_aug, kv_aug],
        out_specs=q_row(0),
        out_shape=jax.ShapeDtypeStruct((batch * seq, FOX_WIDTH), BF16),
        scratch_shapes=[pltpu.VMEM((FOX_HEADS, tq, FOX_HEAD_DIM), F32),
                        pltpu.VMEM((FOX_HEADS, tq, 2 * FOX_HEAD_DIM), F32)],
        compiler_params=_params(("arbitrary", "arbitrary", "arbitrary")),
        name="fox_attention",
    )(h16, h16, h16, qa, ka)


def _layer_norm(z, g, b):
    mu = jnp.mean(z, axis=-1, keepdims=True)
    zc = z - mu
    var = jnp.mean(zc * zc, axis=-1, keepdims=True)
    return zc * lax.rsqrt(var + LN_EPS) * g + b


def _outproj_kernel(yp_ref, yh_ref, yf_ref, w_ref, x_ref, g_ref, b_ref, o32_ref, o16_ref):
    tm = x_ref.shape[0]
    sub = min(tm, SUB_LN)
    for r0 in range(0, tm, sub):
        rows = slice(r0, r0 + sub)
        y = jnp.concatenate([yp_ref[rows, :], yh_ref[rows, :], yf_ref[rows, :]], axis=1)
        z = DEEPNORM_ALPHA * x_ref[rows, :] + jnp.dot(y, w_ref[...], preferred_element_type=F32)
        out = _layer_norm(z, g_ref[...], b_ref[...])
        o32_ref[rows, :] = out
        o16_ref[rows, :] = out.astype(BF16)


def _outproj(yp, yh, yf, w, x, g, b):
    m = x.shape[0]
    tm = TM_OUTPROJ
    row = lambda width: pl.BlockSpec((tm, width), lambda i: (i, 0))
    const = lambda shape: pl.BlockSpec(shape, lambda i: (0, 0))
    return pl.pallas_call(
        _outproj_kernel,
        grid=(m // tm,),
        in_specs=[row(POOL_WIDTH), row(HGRN_WIDTH), row(FOX_WIDTH), const((D_MODEL, D_MODEL)),
                  row(D_MODEL), const((1, D_MODEL)), const((1, D_MODEL))],
        out_specs=[row(D_MODEL), row(D_MODEL)],
        out_shape=[jax.ShapeDtypeStruct((m, D_MODEL), F32), jax.ShapeDtypeStruct((m, D_MODEL), BF16)],
        compiler_params=_params(("arbitrary",)),
        name="outproj_ln",
    )(yp, yh, yf, w, x, g, b)


def _ffn_up_kernel(x_ref, wg_ref, wv_ref, cw_ref, cb_ref, o_ref, gbuf_ref, wgb_ref, wvb_ref, *, tiles_per_seq):
    tm = x_ref.shape[0]
    sub = min(tm, SUB_FFN_UP)
    i = pl.program_id(1)

    @pl.when(i == 0)
    def _():
        wgb_ref[...] = wg_ref[...].astype(BF16)
        wvb_ref[...] = wv_ref[...].astype(BF16)

    @pl.when(i % tiles_per_seq == 0)
    def _():
        gbuf_ref[0:8, :] = jnp.zeros((8, gbuf_ref.shape[1]), F32)

    for r0 in range(0, tm, sub):
        x = x_ref[r0:r0 + sub, :]
        gate = jnp.dot(x, wgb_ref[...], preferred_element_type=F32)
        val = jnp.dot(x, wvb_ref[...], preferred_element_type=F32)
        gbuf_ref[8 + r0:8 + r0 + sub, :] = gate
        conv = (cw_ref[2:3, :] * gate + cw_ref[1:2, :] * gbuf_ref[7 + r0:7 + r0 + sub, :]
                + cw_ref[0:1, :] * gbuf_ref[6 + r0:6 + r0 + sub, :] + cb_ref[...])
        o_ref[r0:r0 + sub, :] = (conv / (1.0 + jnp.exp(-conv)) * val).astype(BF16)
    gbuf_ref[0:8, :] = gbuf_ref[tm:tm + 8, :]


def _ffn_up(xb, w_gate, w_val, layer, cw, cb, seq):
    m = xb.shape[0]
    tm, tn = TM_FFN_UP, TN_FFN_UP
    weight = pl.BlockSpec((None, D_MODEL, tn), lambda j, i: (layer, 0, j))
    return pl.pallas_call(
        functools.partial(_ffn_up_kernel, tiles_per_seq=seq // tm),
        grid=(D_FF // tn, m // tm),
        in_specs=[pl.BlockSpec((tm, D_MODEL), lambda j, i: (i, 0)), weight, weight,
                  pl.BlockSpec((3, tn), lambda j, i: (0, j)),
                  pl.BlockSpec((1, tn), lambda j, i: (0, j))],
        out_specs=pl.BlockSpec((tm, tn), lambda j, i: (i, j)),
        out_shape=jax.ShapeDtypeStruct((m, D_FF), BF16),
        scratch_shapes=[pltpu.VMEM((tm + 8, tn), F32),
                        pltpu.VMEM((D_MODEL, tn), BF16),
                        pltpu.VMEM((D_MODEL, tn), BF16)],
        compiler_params=_params(("arbitrary", "arbitrary")),
        name="ffn_up",
    )(xb, w_gate, w_val, cw, cb)


def _ffn_down_kernel(a_ref, w_ref, x_ref, g_ref, b_ref, o32_ref, *maybe_o16_ref):
    tm = x_ref.shape[0]
    sub = min(tm, SUB_LN)
    for r0 in range(0, tm, sub):
        rows = slice(r0, r0 + sub)
        z = DEEPNORM_ALPHA * x_ref[rows, :] + jnp.dot(a_ref[rows, :], w_ref[...], preferred_element_type=F32)
        out = _layer_norm(z, g_ref[...], b_ref[...])
        o32_ref[rows, :] = out
        for o16_ref in maybe_o16_ref:
            o16_ref[rows, :] = out.astype(BF16)


def _ffn_down(a, w, x, g, b, want_bf16):
    m = x.shape[0]
    tm = TM_FFN_DOWN
    row = lambda width: pl.BlockSpec((tm, width), lambda i: (i, 0))
    const = lambda shape: pl.BlockSpec(shape, lambda i: (0, 0))
    out_specs = [row(D_MODEL)]
    out_shape = [jax.ShapeDtypeStruct((m, D_MODEL), F32)]
    if want_bf16:
        out_specs.append(row(D_MODEL))
        out_shape.append(jax.ShapeDtypeStruct((m, D_MODEL), BF16))
    return pl.pallas_call(
        _ffn_down_kernel,
        grid=(m // tm,),
        in_specs=[row(D_FF),
                  pl.BlockSpec((D_FF, D_MODEL), lambda i: (0, 0), pipeline_mode=pl.Buffered(1)),
                  row(D_MODEL), const((1, D_MODEL)), const((1, D_MODEL))],
        out_specs=out_specs,
        out_shape=out_shape,
        compiler_params=_params(("arbitrary",)),
        name="ffn_down_ln",
    )(a, w, x, g, b)


def kernel(x, w_in, fox_f_bias, pool_w, pool_scale, hgrn_lb_logits, hgrn_norm_g, w_out, ln1_g, ln1_b,
           w_gate, w_val, conv_w, conv_b, w_down, ln2_g, ln2_b):
    batch, seq, _ = x.shape
    m = batch * seq
    x32 = x.reshape(m, D_MODEL)
    xb = x32
    w_in_t = jnp.swapaxes(w_in, 1, 2)
    q_end = H32_COLS + FOX_WIDTH
    for l in range(DEPTH):
        w32 = w_in_t[l, :H32_COLS].astype(BF16)
        w16 = jnp.concatenate([w_in_t[l, H32_COLS:q_end] * FOX_Q_SCALE,
                               w_in_t[l, q_end:H32_COLS + H16_COLS]], axis=0).astype(BF16)
        wff = jnp.pad(w_in_t[l, H32_COLS + H16_COLS:], ((0, FF_PAD - FOX_HEADS), (0, 0))).astype(BF16)
        bias_pad = jnp.pad(fox_f_bias[l], (0, FF_PAD - FOX_HEADS)).reshape(1, FF_PAD)

        h32, yp, h16, qa, ka = _inproj(xb, w32, w16, wff, bias_pad, pool_w[l].astype(BF16),
                                       pool_scale[l].reshape(1, POOL_WIDTH), seq)
        yh = _hgrn(h32, hgrn_lb_logits, hgrn_norm_g[l].reshape(1, HGRN_WIDTH), l, batch, seq)
        yf = _fox(h16, qa, ka, batch, seq)
        x32, xb = _outproj(yp, yh, yf, w_out[l].astype(BF16), x32,
                           ln1_g[l].reshape(1, D_MODEL), ln1_b[l].reshape(1, D_MODEL))
        a = _ffn_up(xb, w_gate, w_val, l, conv_w[l], conv_b[l].reshape(1, D_FF), seq)
        outs = _ffn_down(a, w_down[l].astype(BF16), x32, ln2_g[l].reshape(1, D_MODEL),
                         ln2_b[l].reshape(1, D_MODEL), want_bf16=l + 1 < DEPTH)
        x32 = outs[0]
        if l + 1 < DEPTH:
            xb = outs[1]
    return x32.reshape(batch, seq, D_MODEL)
```

```python
import functools

import jax
import jax.numpy as jnp
import numpy as np
from jax import lax
from jax.experimental import pallas as pl
from jax.experimental.pallas import tpu as pltpu

F32 = jnp.float32
BF16 = jnp.bfloat16

D_MODEL = 2048
DEPTH = 4
POOL_WIDTH = 512
POOL_WINDOWS = (2, 4, 8, 16)
POOL_GROUP = 128
POOL_HALO = 16
HGRN_WIDTH = 512
HGRN_HEAD_DIM = 128
HGRN_HEADS = 4
FOX_WIDTH = 1024
FOX_HEAD_DIM = 128
FOX_HEADS = 8
D_FF = 5632
DEEPNORM_ALPHA = (2 * DEPTH) ** 0.25
LN_EPS = 1e-5
RMS_EPS = 1e-6
MASK_VALUE = -1e30
EXP_CLAMP = 80.0
LOG2_E = 1.4426950408889634
FOX_Q_SCALE = FOX_HEAD_DIM ** -0.5 * LOG2_E

H32_COLS = POOL_WIDTH + 4 * HGRN_WIDTH
H16_COLS = 3 * FOX_WIDTH
FOX_FF_ROWS = 16
FOX_AUG_LANES = 16

DOT_NT = (((1,), (1,)), ((), ()))
DOT_TN = (((0,), (0,)), ((), ()))
LANE = 128
SUBLANES = 8
VMEM_LIMIT = 56 * 1024 * 1024

TM_INPROJ = 512
TS_HGRN = 256
TQ_FOX = 512
TM_OUTPROJ = 512
TM_FFN_UP = 2048
TN_FFN_UP = 512
SUB_FFN_UP = 256
TM_FFN_DOWN = 256
SUB_LN = 128

HGRN_BLOCK = 128


def _params(sem):
    return pltpu.CompilerParams(dimension_semantics=sem, vmem_limit_bytes=VMEM_LIMIT)


def _log_sigmoid(x):
    return jnp.minimum(x, 0.0) - jnp.log1p(jnp.exp(-jnp.abs(x)))


def _split3(x):
    a = x.astype(BF16)
    r = x - a.astype(F32)
    b = r.astype(BF16)
    c = (r - b.astype(F32)).astype(BF16)
    return jnp.concatenate([a, b, c], axis=1)


def _sum3(y, n):
    return y[:, 0:n] + y[:, n:2 * n] + y[:, 2 * n:3 * n]


def _split2(x):
    a = x.astype(BF16)
    b = (x - a.astype(F32)).astype(BF16)
    return jnp.concatenate([a, b], axis=1)


def _sum2(y, n):
    return y[:, 0:n] + y[:, n:2 * n]


def _inproj32_kernel(x_ref, w_ref, pw_ref, ps_ref, h_ref, yp_ref, ext_ref, *, tiles_per_seq):
    tm = x_ref.shape[0]
    i = pl.program_id(0)

    @pl.when(i % tiles_per_seq == 0)
    def _():
        ext_ref[0:POOL_HALO, :] = jnp.zeros((POOL_HALO, POOL_WIDTH), F32)

    x = x_ref[...].astype(BF16)
    u = lax.dot_general(x, w_ref[0:POOL_WIDTH, :], DOT_NT, preferred_element_type=F32)
    h_ref[...] = lax.dot_general(x, w_ref[POOL_WIDTH:, :], DOT_NT, preferred_element_type=F32)

    ext_ref[POOL_HALO:POOL_HALO + tm, :] = u
    pos = ((i % tiles_per_seq) * tm + lax.broadcasted_iota(jnp.int32, (tm, 1), 0)).astype(F32)
    for gi, w in enumerate(POOL_WINDOWS):
        lanes = slice(gi * POOL_GROUP, (gi + 1) * POOL_GROUP)
        ug = u[:, lanes]
        acc = ug
        for j in range(1, w):
            acc = acc + ext_ref[POOL_HALO - j:POOL_HALO - j + tm, lanes]
        count = jnp.minimum(pos + 1.0, float(w))
        d = (acc / count - ug).astype(BF16)
        y = jnp.dot(d, pw_ref[gi], preferred_element_type=F32) * ps_ref[:, lanes]
        yp_ref[:, lanes] = y.astype(BF16)
    ext_ref[0:POOL_HALO, :] = ext_ref[tm:tm + POOL_HALO, :]


def _split3_rows(x):
    a = x.astype(BF16)
    r = x - a.astype(F32)
    b = r.astype(BF16)
    c = (r - b.astype(F32)).astype(BF16)
    return jnp.concatenate([a, b, c], axis=0)


def _inproj16_kernel(x_ref, w_ref, wff_ref, bias_ref, triu_ref, sel_ref, ones_ref,
                     o_ref, qa_ref, ka_ref, carry_ref, *, tiles_per_seq):
    tm = x_ref.shape[0]
    nr = FOX_FF_ROWS

    @pl.when(pl.program_id(0) % tiles_per_seq == 0)
    def _():
        carry_ref[...] = jnp.zeros_like(carry_ref)

    x = x_ref[...].astype(BF16)
    ff_t = lax.dot_general(wff_ref[...], x, DOT_NT, preferred_element_type=F32)
    lf_t = _log_sigmoid(ff_t + bias_ref[...])
    c3 = jnp.dot(_split3_rows(lf_t), triu_ref[...], preferred_element_type=F32)
    c_t = c3[0:nr] + c3[nr:2 * nr] + c3[2 * nr:3 * nr] + carry_ref[:, 0:1]
    carry_ref[...] = jnp.broadcast_to(c_t[:, tm - 1:tm], carry_ref.shape)
    aug = lax.dot_general(_split3_rows(c_t * LOG2_E), sel_ref[...], DOT_TN,
                          preferred_element_type=F32) + ones_ref[...]
    qa_ref[...] = aug[:, :LANE].astype(BF16)
    ka_ref[...] = aug[:, LANE:].astype(BF16)
    o_ref[...] = lax.dot_general(x, w_ref[...], DOT_NT, preferred_element_type=F32).astype(BF16)


def _inproj(xb, w32, w16, wff, bias_pad, pool_w, pool_scale, seq):
    m = xb.shape[0]
    tm = TM_INPROJ
    row = lambda width: pl.BlockSpec((tm, width), lambda i: (i, 0))
    const = lambda shape: pl.BlockSpec(shape, lambda i: (0,) * len(shape))
    h32, yp = pl.pallas_call(
        functools.partial(_inproj32_kernel, tiles_per_seq=seq // tm),
        grid=(m // tm,),
        in_specs=[row(D_MODEL), const((H32_COLS, D_MODEL)),
                  const((len(POOL_WINDOWS), POOL_GROUP, POOL_GROUP)), const((1, POOL_WIDTH))],
        out_specs=[row(4 * HGRN_WIDTH), row(POOL_WIDTH)],
        out_shape=[jax.ShapeDtypeStruct((m, 4 * HGRN_WIDTH), F32),
                   jax.ShapeDtypeStruct((m, POOL_WIDTH), BF16)],
        scratch_shapes=[pltpu.VMEM((tm + POOL_HALO, POOL_WIDTH), F32)],
        compiler_params=_params(("arbitrary",)),
        name="inproj32_pool",
    )(xb, w32, pool_w, pool_scale)
    sel, ones = _fox_aug_constants()
    triu = np.triu(np.ones((tm, tm), np.float32))
    h16, qa, ka = pl.pallas_call(
        functools.partial(_inproj16_kernel, tiles_per_seq=seq // tm),
        grid=(m // tm,),
        in_specs=[row(D_MODEL), const((H16_COLS, D_MODEL)), const((FOX_FF_ROWS, D_MODEL)),
                  const((FOX_FF_ROWS, tm)), const(triu.shape), const(sel.shape), const(ones.shape)],
        out_specs=[row(H16_COLS), row(LANE), row(LANE)],
        out_shape=[jax.ShapeDtypeStruct((m, H16_COLS), BF16),
                   jax.ShapeDtypeStruct((m, LANE), BF16),
                   jax.ShapeDtypeStruct((m, LANE), BF16)],
        scratch_shapes=[pltpu.VMEM((FOX_FF_ROWS, LANE), F32)],
        compiler_params=_params(("arbitrary",)),
        name="inproj16_cumsum",
    )(xb, w16, wff, jnp.broadcast_to(bias_pad, (FOX_FF_ROWS, tm)), jnp.asarray(triu, dtype=BF16),
      jnp.asarray(sel, dtype=BF16), jnp.asarray(ones))
    return h32, yp, h16, qa, ka


def _fox_aug_constants():
    sel = np.zeros((3 * FOX_FF_ROWS, 2 * LANE), np.float32)
    ones = np.zeros((1, 2 * LANE), np.float32)
    for h in range(FOX_HEADS):
        for p in range(3):
            sel[p * FOX_FF_ROWS + h, h * FOX_AUG_LANES + p] = 1.0
            sel[p * FOX_FF_ROWS + h, LANE + h * FOX_AUG_LANES + 3 + p] = -1.0
            ones[0, h * FOX_AUG_LANES + 3 + p] = 1.0
            ones[0, LANE + h * FOX_AUG_LANES + p] = 1.0
    return sel, ones


def _hgrn_levels(c):
    return tuple(c >> k for k in range(1, c.bit_length()))


HGRN_MATRIX_LEVELS = (4, 2)


def _hgrn_level_matrix(c):
    t = np.arange(c)[:, None]
    u = np.arange(c)[None, :]
    mats = [(u <= t)]
    for m in HGRN_MATRIX_LEVELS:
        p = t % (2 * m)
        r = t - p + m - 1
        upper = p >= m
        mats.append(np.where(upper, (u > r) & (u <= t), (u > t) & (u <= r)))
    return np.concatenate(mats, axis=0).astype(np.float32)


def _hgrn_kernel(q_ref, z_ref, v_ref, g_ref, lbl_ref, ng_ref, gm_ref, o_ref, st_ref, *, layer):
    c = q_ref.shape[0]
    d = HGRN_HEAD_DIM
    blk = min(c, HGRN_BLOCK)
    nblk = c // blk
    levels = _hgrn_levels(c)

    @pl.when(pl.program_id(1) == 0)
    def _():
        st_ref[...] = jnp.zeros_like(st_ref)

    lbl = lbl_ref[...]
    e = jnp.exp(lbl - jnp.max(lbl, axis=0, keepdims=True))
    lbw = e / jnp.sum(e, axis=0, keepdims=True)
    lb_all = jnp.zeros((1, HGRN_WIDTH), F32)
    for li in range(1, layer + 1):
        lb_all = lb_all + lbw[li:li + 1, :]

    gm = gm_ref[...]
    row_d = lax.broadcasted_iota(jnp.int32, (c, d), 0)
    row_b = lax.broadcasted_iota(jnp.int32, (blk, blk), 0)
    col_b = lax.broadcasted_iota(jnp.int32, (blk, blk), 1)
    same = row_b ^ col_b
    dn_t = (((1,), (1,)), ((), ()))
    dn_a = (((0,), (0,)), ((), ()))
    rows = lambda bi: slice(bi * blk, (bi + 1) * blk)

    heads = range(HGRN_HEADS)
    lane = lambda h: slice(h * d, (h + 1) * d)
    q = [q_ref[:, lane(h)] for h in heads]
    v = [v_ref[:, lane(h)].astype(BF16) for h in heads]

    kk, lf2 = [], []
    for h in heads:
        z = z_ref[:, lane(h)]
        log_f = jnp.minimum(z, 0.0) - jnp.log(1.0 + jnp.exp(-jnp.abs(z)))
        sig_neg = 1.0 / (1.0 + jnp.exp(z))
        if layer > 0:
            lb = lb_all[:, lane(h)]
            log_f = log_f + jnp.log(1.0 + lb * jnp.exp(jnp.minimum(-z, EXP_CLAMP)))
            sig_neg = (1.0 - lb) * sig_neg
        kk.append(sig_neg)
        lf2.append(jnp.minimum(log_f, 0.0) * LOG2_E)

    ex = [_sum2(jnp.dot(gm, _split2(lf2[h]), preferred_element_type=F32), d) for h in heads]

    o_inter = []
    for h in heads:
        b = ex[h][0:c]
        b_last = b[c - 1:c, :]
        st = st_ref[h]
        qi = (q[h] * jnp.exp2(b)).astype(BF16)
        o_inter.append(lax.dot_general(qi, st.astype(BF16), dn_t, preferred_element_type=F32))
        kd = (kk[h] * jnp.exp2(b_last - b)).astype(BF16)
        st_ref[h] = st * jnp.exp2(b_last) + lax.dot_general(v[h], kd, dn_a, preferred_element_type=F32)

    diag = [[None] * nblk for _ in heads]
    off = [{} for _ in heads]
    for m in levels:
        upper = (row_d & m) != 0
        for h in heads:
            b = ex[h][0:c]
            if m >= SUBLANES:
                parts = []
                for r0 in range(0, c, 2 * m):
                    lo, hi = slice(r0, r0 + m), slice(r0 + m, r0 + 2 * m)
                    b_ref_row = b[r0 + m - 1:r0 + m, :]
                    parts.append(kk[h][lo] * jnp.exp2(b_ref_row - b[lo]))
                    parts.append(q[h][hi] * jnp.exp2(b[hi] - b_ref_row))
                x = jnp.concatenate(parts, axis=0).astype(BF16)
            elif m == 1:
                x = jnp.where(upper, q[h] * jnp.exp2(lf2[h]), kk[h]).astype(BF16)
            else:
                k = 1 + HGRN_MATRIX_LEVELS.index(m)
                x = (jnp.where(upper, q[h], kk[h]) * jnp.exp2(ex[h][k * c:(k + 1) * c])).astype(BF16)
            for bi in range(nblk):
                if m < blk:
                    a = lax.dot_general(x[rows(bi)], x[rows(bi)], dn_t, preferred_element_type=F32)
                    diag[h][bi] = a if diag[h][bi] is None else jnp.where(same < 2 * m, a, diag[h][bi])
                elif (bi * blk) & m:
                    for bj in range(nblk):
                        if not (bj * blk) & m and (bi * blk) // (2 * m) == (bj * blk) // (2 * m):
                            off[h][bi, bj] = lax.dot_general(x[rows(bi)], x[rows(bj)], dn_t,
                                                             preferred_element_type=F32)

    for h in heads:
        qb = q[h].astype(BF16)
        kb = kk[h].astype(BF16)
        g = g_ref[:, lane(h)]
        gate = ng_ref[:, lane(h)] * (g / (1.0 + jnp.exp(-g)))
        for bi in range(nblk):
            a = lax.dot_general(qb[rows(bi)], kb[rows(bi)], dn_t, preferred_element_type=F32)
            att = jnp.where(same == 0, a, diag[h][bi])
            att = jnp.where(row_b >= col_b, att, 0.0)
            o = o_inter[h][rows(bi)] + jnp.dot(att.astype(BF16), v[h][rows(bi)], preferred_element_type=F32)
            for bj in range(bi):
                o = o + jnp.dot(off[h][bi, bj].astype(BF16), v[h][rows(bj)], preferred_element_type=F32)
            o = o * lax.rsqrt(jnp.mean(o * o, axis=-1, keepdims=True) + RMS_EPS)
            o_ref[rows(bi), lane(h)] = (o * gate[rows(bi)]).astype(BF16)


def _hgrn(h32, lb_logits, norm_g, layer, batch, seq):
    ts = TS_HGRN
    nt = seq // ts
    gm = jnp.asarray(_hgrn_level_matrix(ts), dtype=BF16)

    def col(k):
        return pl.BlockSpec((ts, HGRN_WIDTH), lambda b, i: (b * nt + i, k))

    return pl.pallas_call(
        functools.partial(_hgrn_kernel, layer=layer),
        grid=(batch, nt),
        in_specs=[col(0), col(1), col(2), col(3),
                  pl.BlockSpec((DEPTH, HGRN_WIDTH), lambda b, i: (0, 0)),
                  pl.BlockSpec((1, HGRN_WIDTH), lambda b, i: (0, 0)),
                  pl.BlockSpec(gm.shape, lambda b, i: (0, 0))],
        out_specs=pl.BlockSpec((ts, HGRN_WIDTH), lambda b, i: (b * nt + i, 0)),
        out_shape=jax.ShapeDtypeStruct((batch * seq, HGRN_WIDTH), BF16),
        scratch_shapes=[pltpu.VMEM((HGRN_HEADS, HGRN_HEAD_DIM, HGRN_HEAD_DIM), F32)],
        compiler_params=_params(("arbitrary", "arbitrary")),
        name="hgrn_mixer",
    )(h32, h32, h32, h32, lb_logits, norm_g, gm)


def _fox_kernel(q_ref, k_ref, v_ref, qa_ref, ka_ref, o_ref, m_ref, acc_ref):
    tq = q_ref.shape[0]
    tk = k_ref.shape[0]
    i = pl.program_id(1)
    j = pl.program_id(2)
    dh = FOX_HEAD_DIM
    dn_t = (((1,), (1,)), ((), ()))

    @pl.when(j == 0)
    def _():
        m_ref[...] = jnp.full(m_ref.shape, MASK_VALUE, F32)
        acc_ref[...] = jnp.zeros_like(acc_ref)

    def kv_step(on_diagonal):
        if on_diagonal:
            causal = (lax.broadcasted_iota(jnp.int32, (tq, tk), 0)
                      >= lax.broadcasted_iota(jnp.int32, (tq, tk), 1))
        ones = jnp.ones((tk, dh), BF16)
        qa = qa_ref[...]
        ka = ka_ref[...]
        aug_head = lax.broadcasted_iota(jnp.int32, ka.shape, 1) // FOX_AUG_LANES
        scores, row_max, probs, alphas = [], [], [], []
        for h in range(FOX_HEADS):
            lanes = slice(h * dh, (h + 1) * dh)
            qc = jnp.concatenate([q_ref[:, lanes], qa], axis=1)
            kc = jnp.concatenate([k_ref[:, lanes], jnp.where(aug_head == h, ka, jnp.zeros_like(ka))], axis=1)
            s = lax.dot_general(qc, kc, dn_t, preferred_element_type=F32)
            if on_diagonal:
                s = jnp.where(causal, s, MASK_VALUE)
            scores.append(s)
            row_max.append(jnp.max(s, axis=-1, keepdims=True))
        for h in range(FOX_HEADS):
            m_prev = m_ref[h]
            m_new = jnp.maximum(m_prev, jnp.broadcast_to(row_max[h], m_prev.shape))
            alphas.append(jnp.exp2(m_prev - m_new))
            probs.append(jnp.exp2(scores[h] - jnp.concatenate([m_new] * (tk // dh), axis=1)).astype(BF16))
            m_ref[h] = m_new
        for h in range(FOX_HEADS):
            lanes = slice(h * dh, (h + 1) * dh)
            pv = jnp.dot(probs[h], jnp.concatenate([v_ref[:, lanes], ones], axis=1),
                         preferred_element_type=F32)
            acc_ref[h] = jnp.concatenate([alphas[h], alphas[h]], axis=1) * acc_ref[h] + pv

    @pl.when(j < i)
    def _():
        kv_step(False)

    @pl.when(j == i)
    def _():
        kv_step(True)
        for h in range(FOX_HEADS):
            lanes = slice(h * dh, (h + 1) * dh)
            acc = acc_ref[h]
            o_ref[:, lanes] = (acc[:, :dh] / acc[:, dh:]).astype(BF16)


def _fox(h16, qa, ka, batch, seq):
    tq = TQ_FOX
    nq = seq // tq
    q_row = lambda col: pl.BlockSpec((tq, FOX_WIDTH), lambda b, i, j: (b * nq + i, col))
    kv_row = lambda col: pl.BlockSpec((tq, FOX_WIDTH), lambda b, i, j: (b * nq + jnp.minimum(j, i), col))
    q_aug = pl.BlockSpec((tq, LANE), lambda b, i, j: (b * nq + i, 0))
    kv_aug = pl.BlockSpec((tq, LANE), lambda b, i, j: (b * nq + jnp.minimum(j, i), 0))
    return pl.pallas_call(
        _fox_kernel,
        grid=(batch, nq, nq),
        in_specs=[q_row(0), kv_row(1), kv_row(2), q_aug, kv_aug],
        out_specs=q_row(0),
        out_shape=jax.ShapeDtypeStruct((batch * seq, FOX_WIDTH), BF16),
        scratch_shapes=[pltpu.VMEM((FOX_HEADS, tq, FOX_HEAD_DIM), F32),
                        pltpu.VMEM((FOX_HEADS, tq, 2 * FOX_HEAD_DIM), F32)],
        compiler_params=_params(("arbitrary", "arbitrary", "arbitrary")),
        name="fox_attention",
    )(h16, h16, h16, qa, ka)


def _layer_norm(z, g, b):
    mu = jnp.mean(z, axis=-1, keepdims=True)
    zc = z - mu
    var = jnp.mean(zc * zc, axis=-1, keepdims=True)
    return zc * lax.rsqrt(var + LN_EPS) * g + b


def _outproj_kernel(yp_ref, yh_ref, yf_ref, w_ref, x_ref, g_ref, b_ref, o32_ref, o16_ref):
    tm = x_ref.shape[0]
    sub = min(tm, SUB_LN)
    for r0 in range(0, tm, sub):
        rows = slice(r0, r0 + sub)
        y = jnp.concatenate([yp_ref[rows, :], yh_ref[rows, :], yf_ref[rows, :]], axis=1)
        z = DEEPNORM_ALPHA * x_ref[rows, :] + jnp.dot(y, w_ref[...], preferred_element_type=F32)
        out = _layer_norm(z, g_ref[...], b_ref[...])
        o32_ref[rows, :] = out
        o16_ref[rows, :] = out.astype(BF16)


def _outproj(yp, yh, yf, w, x, g, b):
    m = x.shape[0]
    tm = TM_OUTPROJ
    row = lambda width: pl.BlockSpec((tm, width), lambda i: (i, 0))
    const = lambda shape: pl.BlockSpec(shape, lambda i: (0, 0))
    return pl.pallas_call(
        _outproj_kernel,
        grid=(m // tm,),
        in_specs=[row(POOL_WIDTH), row(HGRN_WIDTH), row(FOX_WIDTH), const((D_MODEL, D_MODEL)),
                  row(D_MODEL), const((1, D_MODEL)), const((1, D_MODEL))],
        out_specs=[row(D_MODEL), row(D_MODEL)],
        out_shape=[jax.ShapeDtypeStruct((m, D_MODEL), F32), jax.ShapeDtypeStruct((m, D_MODEL), BF16)],
        compiler_params=_params(("arbitrary",)),
        name="outproj_ln",
    )(yp, yh, yf, w, x, g, b)


def _ffn_up_kernel(x_ref, wg_ref, wv_ref, cw_ref, cb_ref, o_ref, gbuf_ref, wgb_ref, wvb_ref, *, tiles_per_seq):
    tm = x_ref.shape[0]
    sub = min(tm, SUB_FFN_UP)
    i = pl.program_id(1)

    @pl.when(i == 0)
    def _():
        wgb_ref[...] = wg_ref[...].astype(BF16)
        wvb_ref[...] = wv_ref[...].astype(BF16)

    @pl.when(i % tiles_per_seq == 0)
    def _():
        gbuf_ref[0:8, :] = jnp.zeros((8, gbuf_ref.shape[1]), F32)

    for r0 in range(0, tm, sub):
        x = x_ref[r0:r0 + sub, :]
        gate = jnp.dot(x, wgb_ref[...], preferred_element_type=F32)
        val = jnp.dot(x, wvb_ref[...], preferred_element_type=F32)
        gbuf_ref[8 + r0:8 + r0 + sub, :] = gate
        conv = (cw_ref[2:3, :] * gate + cw_ref[1:2, :] * gbuf_ref[7 + r0:7 + r0 + sub, :]
                + cw_ref[0:1, :] * gbuf_ref[6 + r0:6 + r0 + sub, :] + cb_ref[...])
        o_ref[r0:r0 + sub, :] = (conv / (1.0 + jnp.exp(-conv)) * val).astype(BF16)
    gbuf_ref[0:8, :] = gbuf_ref[tm:tm + 8, :]


def _ffn_up(xb, w_gate, w_val, layer, cw, cb, seq):
    m = xb.shape[0]
    tm, tn = TM_FFN_UP, TN_FFN_UP
    weight = pl.BlockSpec((None, D_MODEL, tn), lambda j, i: (layer, 0, j))
    return pl.pallas_call(
        functools.partial(_ffn_up_kernel, tiles_per_seq=seq // tm),
        grid=(D_FF // tn, m // tm),
        in_specs=[pl.BlockSpec((tm, D_MODEL), lambda j, i: (i, 0)), weight, weight,
                  pl.BlockSpec((3, tn), lambda j, i: (0, j)),
                  pl.BlockSpec((1, tn), lambda j, i: (0, j))],
        out_specs=pl.BlockSpec((tm, tn), lambda j, i: (i, j)),
        out_shape=jax.ShapeDtypeStruct((m, D_FF), BF16),
        scratch_shapes=[pltpu.VMEM((tm + 8, tn), F32),
                        pltpu.VMEM((D_MODEL, tn), BF16),
                        pltpu.VMEM((D_MODEL, tn), BF16)],
        compiler_params=_params(("arbitrary", "arbitrary")),
        name="ffn_up",
    )(xb, w_gate, w_val, cw, cb)


def _ffn_down_kernel(a_ref, w_ref, x_ref, g_ref, b_ref, o32_ref, *maybe_o16_ref):
    tm = x_ref.shape[0]
    sub = min(tm, SUB_LN)
    for r0 in range(0, tm, sub):
        rows = slice(r0, r0 + sub)
        z = DEEPNORM_ALPHA * x_ref[rows, :] + jnp.dot(a_ref[rows, :], w_ref[...], preferred_element_type=F32)
        out = _layer_norm(z, g_ref[...], b_ref[...])
        o32_ref[rows, :] = out
        for o16_ref in maybe_o16_ref:
            o16_ref[rows, :] = out.astype(BF16)


def _ffn_down(a, w, x, g, b, want_bf16):
    m = x.shape[0]
    tm = TM_FFN_DOWN
    row = lambda width: pl.BlockSpec((tm, width), lambda i: (i, 0))
    const = lambda shape: pl.BlockSpec(shape, lambda i: (0, 0))
    out_specs = [row(D_MODEL)]
    out_shape = [jax.ShapeDtypeStruct((m, D_MODEL), F32)]
    if want_bf16:
        out_specs.append(row(D_MODEL))
        out_shape.append(jax.ShapeDtypeStruct((m, D_MODEL), BF16))
    return pl.pallas_call(
        _ffn_down_kernel,
        grid=(m // tm,),
        in_specs=[row(D_FF),
                  pl.BlockSpec((D_FF, D_MODEL), lambda i: (0, 0), pipeline_mode=pl.Buffered(1)),
                  row(D_MODEL), const((1, D_MODEL)), const((1, D_MODEL))],
        out_specs=out_specs,
        out_shape=out_shape,
        compiler_params=_params(("arbitrary",)),
        name="ffn_down_ln",
    )(a, w, x, g, b)


def kernel(x, w_in, fox_f_bias, pool_w, pool_scale, hgrn_lb_logits, hgrn_norm_g, w_out, ln1_g, ln1_b,
           w_gate, w_val, conv_w, conv_b, w_down, ln2_g, ln2_b):
    batch, seq, _ = x.shape
    m = batch * seq
    x32 = x.reshape(m, D_MODEL)
    xb = x32
    w_in_t = jnp.swapaxes(w_in, 1, 2)
    q_end = H32_COLS + FOX_WIDTH
    for l in range(DEPTH):
        w32 = w_in_t[l, :H32_COLS].astype(BF16)
        w16 = jnp.concatenate([w_in_t[l, H32_COLS:q_end] * FOX_Q_SCALE,
                               w_in_t[l, q_end:H32_COLS + H16_COLS]], axis=0).astype(BF16)
        wff = jnp.pad(w_in_t[l, H32_COLS + H16_COLS:], ((0, FOX_FF_ROWS - FOX_HEADS), (0, 0))).astype(BF16)
        bias_pad = jnp.pad(fox_f_bias[l], (0, FOX_FF_ROWS - FOX_HEADS)).reshape(FOX_FF_ROWS, 1)

        h32, yp, h16, qa, ka = _inproj(xb, w32, w16, wff, bias_pad, pool_w[l].astype(BF16),
                                       pool_scale[l].reshape(1, POOL_WIDTH), seq)
        yh = _hgrn(h32, hgrn_lb_logits, hgrn_norm_g[l].reshape(1, HGRN_WIDTH), l, batch, seq)
        yf = _fox(h16, qa, ka, batch, seq)
        x32, xb = _outproj(yp, yh, yf, w_out[l].astype(BF16), x32,
                           ln1_g[l].reshape(1, D_MODEL), ln1_b[l].reshape(1, D_MODEL))
        a = _ffn_up(xb, w_gate, w_val, l, conv_w[l], conv_b[l].reshape(1, D_FF), seq)
        outs = _ffn_down(a, w_down[l].astype(BF16), x32, ln2_g[l].reshape(1, D_MODEL),
                         ln2_b[l].reshape(1, D_MODEL), want_bf16=l + 1 < DEPTH)
        x32 = outs[0]
        if l + 1 < DEPTH:
            xb = outs[1]
    return x32.reshape(batch, seq, D_MODEL)
```

```python
import functools

import jax
import jax.numpy as jnp
import numpy as np
from jax import lax
from jax.experimental import pallas as pl
from jax.experimental.pallas import tpu as pltpu

F32 = jnp.float32
BF16 = jnp.bfloat16

D_MODEL = 2048
DEPTH = 4
POOL_WIDTH = 512
POOL_WINDOWS = (2, 4, 8, 16)
POOL_GROUP = 128
POOL_HALO = 16
HGRN_WIDTH = 512
HGRN_HEAD_DIM = 128
HGRN_HEADS = 4
FOX_WIDTH = 1024
FOX_HEAD_DIM = 128
FOX_HEADS = 8
D_FF = 5632
DEEPNORM_ALPHA = (2 * DEPTH) ** 0.25
LN_EPS = 1e-5
RMS_EPS = 1e-6
MASK_VALUE = -1e30
EXP_CLAMP = 80.0
LOG2_E = 1.4426950408889634
FOX_Q_SCALE = FOX_HEAD_DIM ** -0.5 * LOG2_E

H32_COLS = POOL_WIDTH + 4 * HGRN_WIDTH
H16_COLS = 3 * FOX_WIDTH
FOX_FF_ROWS = 16
FOX_AUG_LANES = 16
FOX_SUM_ROWS = 16

DOT_NT = (((1,), (1,)), ((), ()))
DOT_TN = (((0,), (0,)), ((), ()))
LANE = 128
SUBLANES = 8
VMEM_LIMIT = 56 * 1024 * 1024

TM_INPROJ = 512
TS_HGRN = 256
TQ_FOX = 512
TM_OUTPROJ = 512
TM_FFN_UP = 2048
TN_FFN_UP = 512
SUB_FFN_UP = 256
TM_FFN_DOWN = 256
SUB_LN = 128

HGRN_BLOCK = 128


def _params(sem):
    return pltpu.CompilerParams(dimension_semantics=sem, vmem_limit_bytes=VMEM_LIMIT)


def _log_sigmoid(x):
    return jnp.minimum(x, 0.0) - jnp.log1p(jnp.exp(-jnp.abs(x)))


def _split3(x):
    a = x.astype(BF16)
    r = x - a.astype(F32)
    b = r.astype(BF16)
    c = (r - b.astype(F32)).astype(BF16)
    return jnp.concatenate([a, b, c], axis=1)


def _sum3(y, n):
    return y[:, 0:n] + y[:, n:2 * n] + y[:, 2 * n:3 * n]


def _split2(x):
    a = x.astype(BF16)
    b = (x - a.astype(F32)).astype(BF16)
    return jnp.concatenate([a, b], axis=1)


def _sum2(y, n):
    return y[:, 0:n] + y[:, n:2 * n]


def _inproj32_kernel(x_ref, w_ref, pw_ref, ps_ref, h_ref, yp_ref, ext_ref, *, tiles_per_seq):
    tm = x_ref.shape[0]
    i = pl.program_id(0)

    @pl.when(i % tiles_per_seq == 0)
    def _():
        ext_ref[0:POOL_HALO, :] = jnp.zeros((POOL_HALO, POOL_WIDTH), F32)

    x = x_ref[...].astype(BF16)
    u = lax.dot_general(x, w_ref[0:POOL_WIDTH, :], DOT_NT, preferred_element_type=F32)
    h_ref[...] = lax.dot_general(x, w_ref[POOL_WIDTH:, :], DOT_NT, preferred_element_type=F32)

    ext_ref[POOL_HALO:POOL_HALO + tm, :] = u
    pos = ((i % tiles_per_seq) * tm + lax.broadcasted_iota(jnp.int32, (tm, 1), 0)).astype(F32)
    for gi, w in enumerate(POOL_WINDOWS):
        lanes = slice(gi * POOL_GROUP, (gi + 1) * POOL_GROUP)
        ug = u[:, lanes]
        acc = ug
        for j in range(1, w):
            acc = acc + ext_ref[POOL_HALO - j:POOL_HALO - j + tm, lanes]
        count = jnp.minimum(pos + 1.0, float(w))
        d = (acc / count - ug).astype(BF16)
        y = jnp.dot(d, pw_ref[gi], preferred_element_type=F32) * ps_ref[:, lanes]
        yp_ref[:, lanes] = y.astype(BF16)
    ext_ref[0:POOL_HALO, :] = ext_ref[tm:tm + POOL_HALO, :]


def _split3_rows(x):
    a = x.astype(BF16)
    r = x - a.astype(F32)
    b = r.astype(BF16)
    c = (r - b.astype(F32)).astype(BF16)
    return jnp.concatenate([a, b, c], axis=0)


def _inproj16_kernel(x_ref, w_ref, wv_ref, wff_ref, bias_ref, triu_ref, sel_ref, ones_ref,
                     o_ref, vt_ref, qa_ref, ka_ref, carry_ref, *, tiles_per_seq):
    tm = x_ref.shape[0]
    nr = FOX_FF_ROWS

    @pl.when(pl.program_id(0) % tiles_per_seq == 0)
    def _():
        carry_ref[...] = jnp.zeros_like(carry_ref)

    x = x_ref[...].astype(BF16)
    ff_t = lax.dot_general(wff_ref[...], x, DOT_NT, preferred_element_type=F32)
    lf_t = _log_sigmoid(ff_t + bias_ref[...])
    c3 = jnp.dot(_split3_rows(lf_t), triu_ref[...], preferred_element_type=F32)
    c_t = c3[0:nr] + c3[nr:2 * nr] + c3[2 * nr:3 * nr] + carry_ref[:, 0:1]
    carry_ref[...] = jnp.broadcast_to(c_t[:, tm - 1:tm], carry_ref.shape)
    aug = lax.dot_general(_split3_rows(c_t * LOG2_E), sel_ref[...], DOT_TN,
                          preferred_element_type=F32) + ones_ref[...]
    qa_ref[...] = aug[:, :LANE].astype(BF16)
    ka_ref[...] = aug[:, LANE:].astype(BF16)
    o_ref[...] = lax.dot_general(x, w_ref[...], DOT_NT, preferred_element_type=F32).astype(BF16)
    vt_ref[...] = lax.dot_general(wv_ref[...], x, DOT_NT, preferred_element_type=F32).astype(BF16)


def _inproj(xb, w32, wqk, wv, wff, bias_pad, pool_w, pool_scale, seq):
    m = xb.shape[0]
    tm = TM_INPROJ
    row = lambda width: pl.BlockSpec((tm, width), lambda i: (i, 0))
    const = lambda shape: pl.BlockSpec(shape, lambda i: (0,) * len(shape))
    h32, yp = pl.pallas_call(
        functools.partial(_inproj32_kernel, tiles_per_seq=seq // tm),
        grid=(m // tm,),
        in_specs=[row(D_MODEL), const((H32_COLS, D_MODEL)),
                  const((len(POOL_WINDOWS), POOL_GROUP, POOL_GROUP)), const((1, POOL_WIDTH))],
        out_specs=[row(4 * HGRN_WIDTH), row(POOL_WIDTH)],
        out_shape=[jax.ShapeDtypeStruct((m, 4 * HGRN_WIDTH), F32),
                   jax.ShapeDtypeStruct((m, POOL_WIDTH), BF16)],
        scratch_shapes=[pltpu.VMEM((tm + POOL_HALO, POOL_WIDTH), F32)],
        compiler_params=_params(("arbitrary",)),
        name="inproj32_pool",
    )(xb, w32, pool_w, pool_scale)
    sel, ones = _fox_aug_constants()
    triu = np.triu(np.ones((tm, tm), np.float32))
    qk, vt, qa, ka = pl.pallas_call(
        functools.partial(_inproj16_kernel, tiles_per_seq=seq // tm),
        grid=(m // tm,),
        in_specs=[row(D_MODEL), const((2 * FOX_WIDTH, D_MODEL)), const((FOX_WIDTH, D_MODEL)),
                  const((FOX_FF_ROWS, D_MODEL)), const((FOX_FF_ROWS, tm)), const(triu.shape),
                  const(sel.shape), const(ones.shape)],
        out_specs=[row(2 * FOX_WIDTH), pl.BlockSpec((FOX_WIDTH, tm), lambda i: (0, i)), row(LANE), row(LANE)],
        out_shape=[jax.ShapeDtypeStruct((m, 2 * FOX_WIDTH), BF16),
                   jax.ShapeDtypeStruct((FOX_WIDTH, m), BF16),
                   jax.ShapeDtypeStruct((m, LANE), BF16),
                   jax.ShapeDtypeStruct((m, LANE), BF16)],
        scratch_shapes=[pltpu.VMEM((FOX_FF_ROWS, LANE), F32)],
        compiler_params=_params(("arbitrary",)),
        name="inproj16_cumsum",
    )(xb, wqk, wv, wff, jnp.broadcast_to(bias_pad, (FOX_FF_ROWS, tm)), jnp.asarray(triu, dtype=BF16),
      jnp.asarray(sel, dtype=BF16), jnp.asarray(ones))
    return h32, yp, qk, vt, qa, ka


def _fox_aug_constants():
    sel = np.zeros((3 * FOX_FF_ROWS, 2 * LANE), np.float32)
    ones = np.zeros((1, 2 * LANE), np.float32)
    for h in range(FOX_HEADS):
        for p in range(3):
            sel[p * FOX_FF_ROWS + h, h * FOX_AUG_LANES + p] = 1.0
            sel[p * FOX_FF_ROWS + h, LANE + h * FOX_AUG_LANES + 3 + p] = -1.0
            ones[0, h * FOX_AUG_LANES + 3 + p] = 1.0
            ones[0, LANE + h * FOX_AUG_LANES + p] = 1.0
    return sel, ones


def _hgrn_levels(c):
    return tuple(c >> k for k in range(1, c.bit_length()))


HGRN_MATRIX_LEVELS = (4, 2)


def _hgrn_level_matrix(c):
    t = np.arange(c)[:, None]
    u = np.arange(c)[None, :]
    mats = [(u <= t)]
    for m in HGRN_MATRIX_LEVELS:
        p = t % (2 * m)
        r = t - p + m - 1
        upper = p >= m
        mats.append(np.where(upper, (u > r) & (u <= t), (u > t) & (u <= r)))
    return np.concatenate(mats, axis=0).astype(np.float32)


def _hgrn_kernel(q_ref, z_ref, v_ref, g_ref, lbl_ref, ng_ref, gm_ref, o_ref, st_ref, *, layer):
    c = q_ref.shape[0]
    d = HGRN_HEAD_DIM
    blk = min(c, HGRN_BLOCK)
    nblk = c // blk
    levels = _hgrn_levels(c)

    @pl.when(pl.program_id(1) == 0)
    def _():
        st_ref[...] = jnp.zeros_like(st_ref)

    lbl = lbl_ref[...]
    e = jnp.exp(lbl - jnp.max(lbl, axis=0, keepdims=True))
    lbw = e / jnp.sum(e, axis=0, keepdims=True)
    lb_all = jnp.zeros((1, HGRN_WIDTH), F32)
    for li in range(1, layer + 1):
        lb_all = lb_all + lbw[li:li + 1, :]

    gm = gm_ref[...]
    row_d = lax.broadcasted_iota(jnp.int32, (c, d), 0)
    row_b = lax.broadcasted_iota(jnp.int32, (blk, blk), 0)
    col_b = lax.broadcasted_iota(jnp.int32, (blk, blk), 1)
    same = row_b ^ col_b
    dn_t = (((1,), (1,)), ((), ()))
    dn_a = (((0,), (0,)), ((), ()))
    rows = lambda bi: slice(bi * blk, (bi + 1) * blk)

    heads = range(HGRN_HEADS)
    lane = lambda h: slice(h * d, (h + 1) * d)
    q = [q_ref[:, lane(h)] for h in heads]
    v = [v_ref[:, lane(h)].astype(BF16) for h in heads]

    kk, lf2 = [], []
    for h in heads:
        z = z_ref[:, lane(h)]
        log_f = jnp.minimum(z, 0.0) - jnp.log(1.0 + jnp.exp(-jnp.abs(z)))
        sig_neg = 1.0 / (1.0 + jnp.exp(z))
        if layer > 0:
            lb = lb_all[:, lane(h)]
            log_f = log_f + jnp.log(1.0 + lb * jnp.exp(jnp.minimum(-z, EXP_CLAMP)))
            sig_neg = (1.0 - lb) * sig_neg
        kk.append(sig_neg)
        lf2.append(jnp.minimum(log_f, 0.0) * LOG2_E)

    ex = [_sum2(jnp.dot(gm, _split2(lf2[h]), preferred_element_type=F32), d) for h in heads]

    o_inter = []
    for h in heads:
        b = ex[h][0:c]
        b_last = b[c - 1:c, :]
        st = st_ref[h]
        qi = (q[h] * jnp.exp2(b)).astype(BF16)
        o_inter.append(lax.dot_general(qi, st.astype(BF16), dn_t, preferred_element_type=F32))
        kd = (kk[h] * jnp.exp2(b_last - b)).astype(BF16)
        st_ref[h] = st * jnp.exp2(b_last) + lax.dot_general(v[h], kd, dn_a, preferred_element_type=F32)

    diag = [[None] * nblk for _ in heads]
    off = [{} for _ in heads]
    for m in levels:
        upper = (row_d & m) != 0
        for h in heads:
            b = ex[h][0:c]
            if m >= SUBLANES:
                parts = []
                for r0 in range(0, c, 2 * m):
                    lo, hi = slice(r0, r0 + m), slice(r0 + m, r0 + 2 * m)
                    b_ref_row = b[r0 + m - 1:r0 + m, :]
                    parts.append(kk[h][lo] * jnp.exp2(b_ref_row - b[lo]))
                    parts.append(q[h][hi] * jnp.exp2(b[hi] - b_ref_row))
                x = jnp.concatenate(parts, axis=0).astype(BF16)
            elif m == 1:
                x = jnp.where(upper, q[h] * jnp.exp2(lf2[h]), kk[h]).astype(BF16)
            else:
                k = 1 + HGRN_MATRIX_LEVELS.index(m)
                x = (jnp.where(upper, q[h], kk[h]) * jnp.exp2(ex[h][k * c:(k + 1) * c])).astype(BF16)
            for bi in range(nblk):
                if m < blk:
                    a = lax.dot_general(x[rows(bi)], x[rows(bi)], dn_t, preferred_element_type=F32)
                    diag[h][bi] = a if diag[h][bi] is None else jnp.where(same < 2 * m, a, diag[h][bi])
                elif (bi * blk) & m:
                    for bj in range(nblk):
                        if not (bj * blk) & m and (bi * blk) // (2 * m) == (bj * blk) // (2 * m):
                            off[h][bi, bj] = lax.dot_general(x[rows(bi)], x[rows(bj)], dn_t,
                                                             preferred_element_type=F32)

    for h in heads:
        qb = q[h].astype(BF16)
        kb = kk[h].astype(BF16)
        g = g_ref[:, lane(h)]
        gate = ng_ref[:, lane(h)] * (g / (1.0 + jnp.exp(-g)))
        for bi in range(nblk):
            a = lax.dot_general(qb[rows(bi)], kb[rows(bi)], dn_t, preferred_element_type=F32)
            att = jnp.where(same == 0, a, diag[h][bi])
            att = jnp.where(row_b >= col_b, att, 0.0)
            o = o_inter[h][rows(bi)] + jnp.dot(att.astype(BF16), v[h][rows(bi)], preferred_element_type=F32)
            for bj in range(bi):
                o = o + jnp.dot(off[h][bi, bj].astype(BF16), v[h][rows(bj)], preferred_element_type=F32)
            o = o * lax.rsqrt(jnp.mean(o * o, axis=-1, keepdims=True) + RMS_EPS)
            o_ref[rows(bi), lane(h)] = (o * gate[rows(bi)]).astype(BF16)


def _hgrn(h32, lb_logits, norm_g, layer, batch, seq):
    ts = TS_HGRN
    nt = seq // ts
    gm = jnp.asarray(_hgrn_level_matrix(ts), dtype=BF16)

    def col(k):
        return pl.BlockSpec((ts, HGRN_WIDTH), lambda b, i: (b * nt + i, k))

    return pl.pallas_call(
        functools.partial(_hgrn_kernel, layer=layer),
        grid=(batch, nt),
        in_specs=[col(0), col(1), col(2), col(3),
                  pl.BlockSpec((DEPTH, HGRN_WIDTH), lambda b, i: (0, 0)),
                  pl.BlockSpec((1, HGRN_WIDTH), lambda b, i: (0, 0)),
                  pl.BlockSpec(gm.shape, lambda b, i: (0, 0))],
        out_specs=pl.BlockSpec((ts, HGRN_WIDTH), lambda b, i: (b * nt + i, 0)),
        out_shape=jax.ShapeDtypeStruct((batch * seq, HGRN_WIDTH), BF16),
        scratch_shapes=[pltpu.VMEM((HGRN_HEADS, HGRN_HEAD_DIM, HGRN_HEAD_DIM), F32)],
        compiler_params=_params(("arbitrary", "arbitrary")),
        name="hgrn_mixer",
    )(h32, h32, h32, h32, lb_logits, norm_g, gm)


def _fox_kernel(q_ref, k_ref, vt_ref, qa_ref, ka_ref, o_ref, m_ref, acc_ref):
    tq = q_ref.shape[0]
    tk = k_ref.shape[0]
    i = pl.program_id(1)
    j = pl.program_id(2)
    dh = FOX_HEAD_DIM

    @pl.when(j == 0)
    def _():
        m_ref[...] = jnp.full(m_ref.shape, MASK_VALUE, F32)
        acc_ref[...] = jnp.zeros_like(acc_ref)

    def kv_step(on_diagonal):
        if on_diagonal:
            causal = (lax.broadcasted_iota(jnp.int32, (tk, tq), 1)
                      >= lax.broadcasted_iota(jnp.int32, (tk, tq), 0))
        ones = jnp.ones((FOX_SUM_ROWS, tk), BF16)
        qa = qa_ref[...]
        ka = ka_ref[...]
        aug_head = lax.broadcasted_iota(jnp.int32, ka.shape, 1) // FOX_AUG_LANES
        scores, col_max, probs, alphas = [], [], [], []
        for h in range(FOX_HEADS):
            lanes = slice(h * dh, (h + 1) * dh)
            qc = jnp.concatenate([q_ref[:, lanes], qa], axis=1)
            kc = jnp.concatenate([k_ref[:, lanes], jnp.where(aug_head == h, ka, jnp.zeros_like(ka))], axis=1)
            s_t = lax.dot_general(kc, qc, DOT_NT, preferred_element_type=F32)
            if on_diagonal:
                s_t = jnp.where(causal, s_t, MASK_VALUE)
            scores.append(s_t)
            col_max.append(jnp.max(s_t, axis=0, keepdims=True))
        for h in range(FOX_HEADS):
            m_prev = m_ref[h]
            m_new = jnp.maximum(m_prev, jnp.broadcast_to(col_max[h], m_prev.shape))
            alphas.append(jnp.exp2(m_prev - m_new)[0:1, :])
            probs.append(jnp.exp2(scores[h] - m_new[0:1, :]).astype(BF16))
            m_ref[h] = m_new
        for h in range(FOX_HEADS):
            lhs = jnp.concatenate([vt_ref[h * dh:(h + 1) * dh, :], ones], axis=0)
            acc_ref[h] = alphas[h] * acc_ref[h] + jnp.dot(lhs, probs[h], preferred_element_type=F32)

    @pl.when(j < i)
    def _():
        kv_step(False)

    @pl.when(j == i)
    def _():
        kv_step(True)
        for h in range(FOX_HEADS):
            acc = acc_ref[h]
            o_ref[h * dh:(h + 1) * dh, :] = (acc[0:dh, :] / acc[dh:dh + 1, :]).astype(BF16)


def _fox(qk, vt, qa, ka, batch, seq):
    tq = TQ_FOX
    nq = seq // tq
    kv = lambda b, i, j: b * nq + jnp.minimum(j, i)
    return pl.pallas_call(
        _fox_kernel,
        grid=(batch, nq, nq),
        in_specs=[pl.BlockSpec((tq, FOX_WIDTH), lambda b, i, j: (b * nq + i, 0)),
                  pl.BlockSpec((tq, FOX_WIDTH), lambda b, i, j: (kv(b, i, j), 1)),
                  pl.BlockSpec((FOX_WIDTH, tq), lambda b, i, j: (0, kv(b, i, j))),
                  pl.BlockSpec((tq, LANE), lambda b, i, j: (b * nq + i, 0)),
                  pl.BlockSpec((tq, LANE), lambda b, i, j: (kv(b, i, j), 0))],
        out_specs=pl.BlockSpec((FOX_WIDTH, tq), lambda b, i, j: (0, b * nq + i)),
        out_shape=jax.ShapeDtypeStruct((FOX_WIDTH, batch * seq), BF16),
        scratch_shapes=[pltpu.VMEM((FOX_HEADS, SUBLANES, tq), F32),
                        pltpu.VMEM((FOX_HEADS, FOX_HEAD_DIM + FOX_SUM_ROWS, tq), F32)],
        compiler_params=_params(("arbitrary", "arbitrary", "arbitrary")),
        name="fox_attention",
    )(qk, qk, vt, qa, ka)


def _layer_norm(z, g, b):
    mu = jnp.mean(z, axis=-1, keepdims=True)
    zc = z - mu
    var = jnp.mean(zc * zc, axis=-1, keepdims=True)
    return zc * lax.rsqrt(var + LN_EPS) * g + b


def _outproj_kernel(yp_ref, yh_ref, yf_ref, w_ref, x_ref, g_ref, b_ref, o32_ref, o16_ref):
    tm = x_ref.shape[0]
    sub = min(tm, SUB_LN)
    n_tok = POOL_WIDTH + HGRN_WIDTH
    for r0 in range(0, tm, sub):
        rows = slice(r0, r0 + sub)
        y = jnp.concatenate([yp_ref[rows, :], yh_ref[rows, :]], axis=1)
        z = (DEEPNORM_ALPHA * x_ref[rows, :]
             + jnp.dot(y, w_ref[0:n_tok, :], preferred_element_type=F32)
             + lax.dot_general(yf_ref[:, rows], w_ref[n_tok:, :], DOT_TN, preferred_element_type=F32))
        out = _layer_norm(z, g_ref[...], b_ref[...])
        o32_ref[rows, :] = out
        o16_ref[rows, :] = out.astype(BF16)


def _outproj(yp, yh, yf, w, x, g, b):
    m = x.shape[0]
    tm = TM_OUTPROJ
    row = lambda width: pl.BlockSpec((tm, width), lambda i: (i, 0))
    const = lambda shape: pl.BlockSpec(shape, lambda i: (0, 0))
    return pl.pallas_call(
        _outproj_kernel,
        grid=(m // tm,),
        in_specs=[row(POOL_WIDTH), row(HGRN_WIDTH), pl.BlockSpec((FOX_WIDTH, tm), lambda i: (0, i)),
                  const((D_MODEL, D_MODEL)), row(D_MODEL), const((1, D_MODEL)), const((1, D_MODEL))],
        out_specs=[row(D_MODEL), row(D_MODEL)],
        out_shape=[jax.ShapeDtypeStruct((m, D_MODEL), F32), jax.ShapeDtypeStruct((m, D_MODEL), BF16)],
        compiler_params=_params(("arbitrary",)),
        name="outproj_ln",
    )(yp, yh, yf, w, x, g, b)


def _ffn_up_kernel(x_ref, wg_ref, wv_ref, cw_ref, cb_ref, o_ref, gbuf_ref, wgb_ref, wvb_ref, *, tiles_per_seq):
    tm = x_ref.shape[0]
    sub = min(tm, SUB_FFN_UP)
    i = pl.program_id(1)

    @pl.when(i == 0)
    def _():
        wgb_ref[...] = wg_ref[...].astype(BF16)
        wvb_ref[...] = wv_ref[...].astype(BF16)

    @pl.when(i % tiles_per_seq == 0)
    def _():
        gbuf_ref[0:8, :] = jnp.zeros((8, gbuf_ref.shape[1]), F32)

    for r0 in range(0, tm, sub):
        x = x_ref[r0:r0 + sub, :]
        gate = jnp.dot(x, wgb_ref[...], preferred_element_type=F32)
        val = jnp.dot(x, wvb_ref[...], preferred_element_type=F32)
        gbuf_ref[8 + r0:8 + r0 + sub, :] = gate
        conv = (cw_ref[2:3, :] * gate + cw_ref[1:2, :] * gbuf_ref[7 + r0:7 + r0 + sub, :]
                + cw_ref[0:1, :] * gbuf_ref[6 + r0:6 + r0 + sub, :] + cb_ref[...])
        o_ref[r0:r0 + sub, :] = (conv / (1.0 + jnp.exp(-conv)) * val).astype(BF16)
    gbuf_ref[0:8, :] = gbuf_ref[tm:tm + 8, :]


def _ffn_up(xb, w_gate, w_val, layer, cw, cb, seq):
    m = xb.shape[0]
    tm, tn = TM_FFN_UP, TN_FFN_UP
    weight = pl.BlockSpec((None, D_MODEL, tn), lambda j, i: (layer, 0, j))
    return pl.pallas_call(
        functools.partial(_ffn_up_kernel, tiles_per_seq=seq // tm),
        grid=(D_FF // tn, m // tm),
        in_specs=[pl.BlockSpec((tm, D_MODEL), lambda j, i: (i, 0)), weight, weight,
                  pl.BlockSpec((3, tn), lambda j, i: (0, j)),
                  pl.BlockSpec((1, tn), lambda j, i: (0, j))],
        out_specs=pl.BlockSpec((tm, tn), lambda j, i: (i, j)),
        out_shape=jax.ShapeDtypeStruct((m, D_FF), BF16),
        scratch_shapes=[pltpu.VMEM((tm + 8, tn), F32),
                        pltpu.VMEM((D_MODEL, tn), BF16),
                        pltpu.VMEM((D_MODEL, tn), BF16)],
        compiler_params=_params(("arbitrary", "arbitrary")),
        name="ffn_up",
    )(xb, w_gate, w_val, cw, cb)


def _ffn_down_kernel(a_ref, w_ref, x_ref, g_ref, b_ref, o32_ref, *maybe_o16_ref):
    tm = x_ref.shape[0]
    sub = min(tm, SUB_LN)
    for r0 in range(0, tm, sub):
        rows = slice(r0, r0 + sub)
        z = DEEPNORM_ALPHA * x_ref[rows, :] + jnp.dot(a_ref[rows, :], w_ref[...], preferred_element_type=F32)
        out = _layer_norm(z, g_ref[...], b_ref[...])
        o32_ref[rows, :] = out
        for o16_ref in maybe_o16_ref:
            o16_ref[rows, :] = out.astype(BF16)


def _ffn_down(a, w, x, g, b, want_bf16):
    m = x.shape[0]
    tm = TM_FFN_DOWN
    row = lambda width: pl.BlockSpec((tm, width), lambda i: (i, 0))
    const = lambda shape: pl.BlockSpec(shape, lambda i: (0, 0))
    out_specs = [row(D_MODEL)]
    out_shape = [jax.ShapeDtypeStruct((m, D_MODEL), F32)]
    if want_bf16:
        out_specs.append(row(D_MODEL))
        out_shape.append(jax.ShapeDtypeStruct((m, D_MODEL), BF16))
    return pl.pallas_call(
        _ffn_down_kernel,
        grid=(m // tm,),
        in_specs=[row(D_FF),
                  pl.BlockSpec((D_FF, D_MODEL), lambda i: (0, 0), pipeline_mode=pl.Buffered(1)),
                  row(D_MODEL), const((1, D_MODEL)), const((1, D_MODEL))],
        out_specs=out_specs,
        out_shape=out_shape,
        compiler_params=_params(("arbitrary",)),
        name="ffn_down_ln",
    )(a, w, x, g, b)


def kernel(x, w_in, fox_f_bias, pool_w, pool_scale, hgrn_lb_logits, hgrn_norm_g, w_out, ln1_g, ln1_b,
           w_gate, w_val, conv_w, conv_b, w_down, ln2_g, ln2_b):
    batch, seq, _ = x.shape
    m = batch * seq
    x32 = x.reshape(m, D_MODEL)
    xb = x32
    w_in_t = jnp.swapaxes(w_in, 1, 2)
    q_end = H32_COLS + FOX_WIDTH
    for l in range(DEPTH):
        w32 = w_in_t[l, :H32_COLS].astype(BF16)
        k_end = q_end + FOX_WIDTH
        wqk = jnp.concatenate([w_in_t[l, H32_COLS:q_end] * FOX_Q_SCALE, w_in_t[l, q_end:k_end]],
                              axis=0).astype(BF16)
        wv = w_in_t[l, k_end:H32_COLS + H16_COLS].astype(BF16)
        wff = jnp.pad(w_in_t[l, H32_COLS + H16_COLS:], ((0, FOX_FF_ROWS - FOX_HEADS), (0, 0))).astype(BF16)
        bias_pad = jnp.pad(fox_f_bias[l], (0, FOX_FF_ROWS - FOX_HEADS)).reshape(FOX_FF_ROWS, 1)

        h32, yp, qk, vt, qa, ka = _inproj(xb, w32, wqk, wv, wff, bias_pad, pool_w[l].astype(BF16),
                                          pool_scale[l].reshape(1, POOL_WIDTH), seq)
        yh = _hgrn(h32, hgrn_lb_logits, hgrn_norm_g[l].reshape(1, HGRN_WIDTH), l, batch, seq)
        yf = _fox(qk, vt, qa, ka, batch, seq)
        x32, xb = _outproj(yp, yh, yf, w_out[l].astype(BF16), x32,
                           ln1_g[l].reshape(1, D_MODEL), ln1_b[l].reshape(1, D_MODEL))
        a = _ffn_up(xb, w_gate, w_val, l, conv_w[l], conv_b[l].reshape(1, D_FF), seq)
        outs = _ffn_down(a, w_down[l].astype(BF16), x32, ln2_g[l].reshape(1, D_MODEL),
                         ln2_b[l].reshape(1, D_MODEL), want_bf16=l + 1 < DEPTH)
        x32 = outs[0]
        if l + 1 < DEPTH:
            xb = outs[1]
    return x32.reshape(batch, seq, D_MODEL)
```

```python
import functools

import jax
import jax.numpy as jnp
import numpy as np
from jax import lax
from jax.experimental import pallas as pl
from jax.experimental.pallas import tpu as pltpu

F32 = jnp.float32
BF16 = jnp.bfloat16

D_MODEL = 2048
DEPTH = 4
POOL_WIDTH = 512
POOL_WINDOWS = (2, 4, 8, 16)
POOL_GROUP = 128
POOL_HALO = 16
HGRN_WIDTH = 512
HGRN_HEAD_DIM = 128
HGRN_HEADS = 4
FOX_WIDTH = 1024
FOX_HEAD_DIM = 128
FOX_HEADS = 8
D_FF = 5632
DEEPNORM_ALPHA = (2 * DEPTH) ** 0.25
LN_EPS = 1e-5
RMS_EPS = 1e-6
MASK_VALUE = -1e30
EXP_CLAMP = 80.0
LOG2_E = 1.4426950408889634
FOX_Q_SCALE = FOX_HEAD_DIM ** -0.5 * LOG2_E

H32_COLS = POOL_WIDTH + 4 * HGRN_WIDTH
H16_COLS = 3 * FOX_WIDTH
FOX_FF_ROWS = 16
FOX_AUG_LANES = 16
FOX_SUM_ROWS = 16

DOT_NT = (((1,), (1,)), ((), ()))
DOT_TN = (((0,), (0,)), ((), ()))
LANE = 128
SUBLANES = 8
VMEM_LIMIT = 56 * 1024 * 1024

TM_INPROJ = 512
TS_HGRN = 256
TQ_FOX = 512
TM_OUTPROJ = 512
TM_FFN_UP = 2048
TN_FFN_UP = 512
SUB_FFN_UP = 256
TM_FFN_DOWN = 256
SUB_LN = 128

HGRN_BLOCK = 128


def _params(sem):
    return pltpu.CompilerParams(dimension_semantics=sem, vmem_limit_bytes=VMEM_LIMIT)


def _log_sigmoid(x):
    return jnp.minimum(x, 0.0) - jnp.log1p(jnp.exp(-jnp.abs(x)))


def _split2(x):
    a = x.astype(BF16)
    b = (x - a.astype(F32)).astype(BF16)
    return jnp.concatenate([a, b], axis=1)


def _sum2(y, n):
    return y[:, 0:n] + y[:, n:2 * n]


def _inproj32_kernel(x_ref, w_ref, pw_ref, ps_ref, h_ref, yp_ref, ext_ref, *, tiles_per_seq):
    tm = x_ref.shape[0]
    i = pl.program_id(0)

    @pl.when(i % tiles_per_seq == 0)
    def _():
        ext_ref[0:POOL_HALO, :] = jnp.zeros((POOL_HALO, POOL_WIDTH), F32)

    x = x_ref[...].astype(BF16)
    u = lax.dot_general(x, w_ref[0:POOL_WIDTH, :], DOT_NT, preferred_element_type=F32)
    h_ref[...] = lax.dot_general(x, w_ref[POOL_WIDTH:, :], DOT_NT, preferred_element_type=F32)

    ext_ref[POOL_HALO:POOL_HALO + tm, :] = u
    pos = ((i % tiles_per_seq) * tm + lax.broadcasted_iota(jnp.int32, (tm, 1), 0)).astype(F32)
    for gi, w in enumerate(POOL_WINDOWS):
        lanes = slice(gi * POOL_GROUP, (gi + 1) * POOL_GROUP)
        ug = u[:, lanes]
        acc = ug
        for j in range(1, w):
            acc = acc + ext_ref[POOL_HALO - j:POOL_HALO - j + tm, lanes]
        count = jnp.minimum(pos + 1.0, float(w))
        d = (acc / count - ug).astype(BF16)
        y = jnp.dot(d, pw_ref[gi], preferred_element_type=F32) * ps_ref[:, lanes]
        yp_ref[:, lanes] = y.astype(BF16)
    ext_ref[0:POOL_HALO, :] = ext_ref[tm:tm + POOL_HALO, :]


def _split3_rows(x):
    a = x.astype(BF16)
    r = x - a.astype(F32)
    b = r.astype(BF16)
    c = (r - b.astype(F32)).astype(BF16)
    return jnp.concatenate([a, b, c], axis=0)


def _inproj16_kernel(x_ref, w_ref, wv_ref, wff_ref, bias_ref, triu_ref, sel_ref, ones_ref,
                     o_ref, vt_ref, qa_ref, ka_ref, carry_ref, *, tiles_per_seq):
    tm = x_ref.shape[0]
    nr = FOX_FF_ROWS

    @pl.when(pl.program_id(0) % tiles_per_seq == 0)
    def _():
        carry_ref[...] = jnp.zeros_like(carry_ref)

    x = x_ref[...].astype(BF16)
    ff_t = lax.dot_general(wff_ref[...], x, DOT_NT, preferred_element_type=F32)
    lf_t = _log_sigmoid(ff_t + bias_ref[...])
    c3 = jnp.dot(_split3_rows(lf_t), triu_ref[...], preferred_element_type=F32)
    c_t = c3[0:nr] + c3[nr:2 * nr] + c3[2 * nr:3 * nr] + carry_ref[:, 0:1]
    carry_ref[...] = jnp.broadcast_to(c_t[:, tm - 1:tm], carry_ref.shape)
    aug = lax.dot_general(_split3_rows(c_t * LOG2_E), sel_ref[...], DOT_TN,
                          preferred_element_type=F32) + ones_ref[...]
    qa_ref[...] = aug[:, :LANE].astype(BF16)
    ka_ref[...] = aug[:, LANE:].astype(BF16)
    o_ref[...] = lax.dot_general(x, w_ref[...], DOT_NT, preferred_element_type=F32).astype(BF16)
    vt_ref[...] = lax.dot_general(wv_ref[...], x, DOT_NT, preferred_element_type=F32).astype(BF16)


def _inproj(xb, w32, wqk, wv, wff, bias_pad, pool_w, pool_scale, seq):
    m = xb.shape[0]
    tm = TM_INPROJ
    row = lambda width: pl.BlockSpec((tm, width), lambda i: (i, 0))
    const = lambda shape: pl.BlockSpec(shape, lambda i: (0,) * len(shape))
    h32, yp = pl.pallas_call(
        functools.partial(_inproj32_kernel, tiles_per_seq=seq // tm),
        grid=(m // tm,),
        in_specs=[row(D_MODEL), const((H32_COLS, D_MODEL)),
                  const((len(POOL_WINDOWS), POOL_GROUP, POOL_GROUP)), const((1, POOL_WIDTH))],
        out_specs=[row(4 * HGRN_WIDTH), row(POOL_WIDTH)],
        out_shape=[jax.ShapeDtypeStruct((m, 4 * HGRN_WIDTH), F32),
                   jax.ShapeDtypeStruct((m, POOL_WIDTH), BF16)],
        scratch_shapes=[pltpu.VMEM((tm + POOL_HALO, POOL_WIDTH), F32)],
        compiler_params=_params(("arbitrary",)),
        name="inproj32_pool",
    )(xb, w32, pool_w, pool_scale)
    sel, ones = _fox_aug_constants()
    triu = np.triu(np.ones((tm, tm), np.float32))
    qk, vt, qa, ka = pl.pallas_call(
        functools.partial(_inproj16_kernel, tiles_per_seq=seq // tm),
        grid=(m // tm,),
        in_specs=[row(D_MODEL), const((2 * FOX_WIDTH, D_MODEL)), const((FOX_WIDTH, D_MODEL)),
                  const((FOX_FF_ROWS, D_MODEL)), const((FOX_FF_ROWS, tm)), const(triu.shape),
                  const(sel.shape), const(ones.shape)],
        out_specs=[row(2 * FOX_WIDTH), pl.BlockSpec((FOX_WIDTH, tm), lambda i: (0, i)), row(LANE), row(LANE)],
        out_shape=[jax.ShapeDtypeStruct((m, 2 * FOX_WIDTH), BF16),
                   jax.ShapeDtypeStruct((FOX_WIDTH, m), BF16),
                   jax.ShapeDtypeStruct((m, LANE), BF16),
                   jax.ShapeDtypeStruct((m, LANE), BF16)],
        scratch_shapes=[pltpu.VMEM((FOX_FF_ROWS, LANE), F32)],
        compiler_params=_params(("arbitrary",)),
        name="inproj16_cumsum",
    )(xb, wqk, wv, wff, jnp.broadcast_to(bias_pad, (FOX_FF_ROWS, tm)), jnp.asarray(triu, dtype=BF16),
      jnp.asarray(sel, dtype=BF16), jnp.asarray(ones))
    return h32, yp, qk, vt, qa, ka


def _fox_aug_constants():
    sel = np.zeros((3 * FOX_FF_ROWS, 2 * LANE), np.float32)
    ones = np.zeros((1, 2 * LANE), np.float32)
    for h in range(FOX_HEADS):
        for p in range(3):
            sel[p * FOX_FF_ROWS + h, h * FOX_AUG_LANES + p] = 1.0
            sel[p * FOX_FF_ROWS + h, LANE + h * FOX_AUG_LANES + 3 + p] = -1.0
            ones[0, h * FOX_AUG_LANES + 3 + p] = 1.0
            ones[0, LANE + h * FOX_AUG_LANES + p] = 1.0
    return sel, ones


def _hgrn_levels(c):
    return tuple(c >> k for k in range(1, c.bit_length()))


HGRN_MATRIX_LEVELS = (4, 2)


def _hgrn_level_matrix(c):
    t = np.arange(c)[:, None]
    u = np.arange(c)[None, :]
    mats = [(u <= t)]
    for m in HGRN_MATRIX_LEVELS:
        p = t % (2 * m)
        r = t - p + m - 1
        upper = p >= m
        mats.append(np.where(upper, (u > r) & (u <= t), (u > t) & (u <= r)))
    return np.concatenate(mats, axis=0).astype(np.float32)


def _hgrn_kernel(q_ref, z_ref, v_ref, g_ref, lbl_ref, ng_ref, gm_ref, o_ref, st_ref, *, layer):
    c = q_ref.shape[0]
    d = HGRN_HEAD_DIM
    blk = min(c, HGRN_BLOCK)
    nblk = c // blk
    levels = _hgrn_levels(c)

    @pl.when(pl.program_id(1) == 0)
    def _():
        st_ref[...] = jnp.zeros_like(st_ref)

    lbl = lbl_ref[...]
    e = jnp.exp(lbl - jnp.max(lbl, axis=0, keepdims=True))
    lbw = e / jnp.sum(e, axis=0, keepdims=True)
    lb_all = jnp.zeros((1, HGRN_WIDTH), F32)
    for li in range(1, layer + 1):
        lb_all = lb_all + lbw[li:li + 1, :]

    gm = gm_ref[...]
    row_d = lax.broadcasted_iota(jnp.int32, (c, d), 0)
    row_b = lax.broadcasted_iota(jnp.int32, (blk, blk), 0)
    col_b = lax.broadcasted_iota(jnp.int32, (blk, blk), 1)
    same = row_b ^ col_b
    dn_t = (((1,), (1,)), ((), ()))
    dn_a = (((0,), (0,)), ((), ()))
    rows = lambda bi: slice(bi * blk, (bi + 1) * blk)

    heads = range(HGRN_HEADS)
    lane = lambda h: slice(h * d, (h + 1) * d)
    q = [q_ref[:, lane(h)] for h in heads]
    v = [v_ref[:, lane(h)].astype(BF16) for h in heads]

    kk, lf2 = [], []
    for h in heads:
        z = z_ref[:, lane(h)]
        log_f = jnp.minimum(z, 0.0) - jnp.log(1.0 + jnp.exp(-jnp.abs(z)))
        sig_neg = 1.0 / (1.0 + jnp.exp(z))
        if layer > 0:
            lb = lb_all[:, lane(h)]
            log_f = log_f + jnp.log(1.0 + lb * jnp.exp(jnp.minimum(-z, EXP_CLAMP)))
            sig_neg = (1.0 - lb) * sig_neg
        kk.append(sig_neg)
        lf2.append(jnp.minimum(log_f, 0.0) * LOG2_E)

    ex = [_sum2(jnp.dot(gm, _split2(lf2[h]), preferred_element_type=F32), d) for h in heads]

    o_inter = []
    for h in heads:
        b = ex[h][0:c]
        b_last = b[c - 1:c, :]
        st = st_ref[h]
        qi = (q[h] * jnp.exp2(b)).astype(BF16)
        o_inter.append(lax.dot_general(qi, st.astype(BF16), dn_t, preferred_element_type=F32))
        kd = (kk[h] * jnp.exp2(b_last - b)).astype(BF16)
        st_ref[h] = st * jnp.exp2(b_last) + lax.dot_general(v[h], kd, dn_a, preferred_element_type=F32)

    diag = [[None] * nblk for _ in heads]
    off = [{} for _ in heads]
    for m in levels:
        upper = (row_d & m) != 0
        for h in heads:
            b = ex[h][0:c]
            if m >= SUBLANES:
                parts = []
                for r0 in range(0, c, 2 * m):
                    lo, hi = slice(r0, r0 + m), slice(r0 + m, r0 + 2 * m)
                    b_ref_row = b[r0 + m - 1:r0 + m, :]
                    parts.append(kk[h][lo] * jnp.exp2(b_ref_row - b[lo]))
                    parts.append(q[h][hi] * jnp.exp2(b[hi] - b_ref_row))
                x = jnp.concatenate(parts, axis=0).astype(BF16)
            elif m == 1:
                x = jnp.where(upper, q[h] * jnp.exp2(lf2[h]), kk[h]).astype(BF16)
            else:
                k = 1 + HGRN_MATRIX_LEVELS.index(m)
                x = (jnp.where(upper, q[h], kk[h]) * jnp.exp2(ex[h][k * c:(k + 1) * c])).astype(BF16)
            for bi in range(nblk):
                if m < blk:
                    a = lax.dot_general(x[rows(bi)], x[rows(bi)], dn_t, preferred_element_type=F32)
                    diag[h][bi] = a if diag[h][bi] is None else jnp.where(same < 2 * m, a, diag[h][bi])
                elif (bi * blk) & m:
                    for bj in range(nblk):
                        if not (bj * blk) & m and (bi * blk) // (2 * m) == (bj * blk) // (2 * m):
                            off[h][bi, bj] = lax.dot_general(x[rows(bi)], x[rows(bj)], dn_t,
                                                             preferred_element_type=F32)

    for h in heads:
        qb = q[h].astype(BF16)
        kb = kk[h].astype(BF16)
        g = g_ref[:, lane(h)]
        gate = ng_ref[:, lane(h)] * (g / (1.0 + jnp.exp(-g)))
        for bi in range(nblk):
            a = lax.dot_general(qb[rows(bi)], kb[rows(bi)], dn_t, preferred_element_type=F32)
            att = jnp.where(same == 0, a, diag[h][bi])
            att = jnp.where(row_b >= col_b, att, 0.0)
            o = o_inter[h][rows(bi)] + jnp.dot(att.astype(BF16), v[h][rows(bi)], preferred_element_type=F32)
            for bj in range(bi):
                o = o + jnp.dot(off[h][bi, bj].astype(BF16), v[h][rows(bj)], preferred_element_type=F32)
            o = o * lax.rsqrt(jnp.mean(o * o, axis=-1, keepdims=True) + RMS_EPS)
            o_ref[rows(bi), lane(h)] = (o * gate[rows(bi)]).astype(BF16)


def _hgrn(h32, lb_logits, norm_g, layer, batch, seq):
    ts = TS_HGRN
    nt = seq // ts
    gm = jnp.asarray(_hgrn_level_matrix(ts), dtype=BF16)

    def col(k):
        return pl.BlockSpec((ts, HGRN_WIDTH), lambda b, i: (b * nt + i, k))

    return pl.pallas_call(
        functools.partial(_hgrn_kernel, layer=layer),
        grid=(batch, nt),
        in_specs=[col(0), col(1), col(2), col(3),
                  pl.BlockSpec((DEPTH, HGRN_WIDTH), lambda b, i: (0, 0)),
                  pl.BlockSpec((1, HGRN_WIDTH), lambda b, i: (0, 0)),
                  pl.BlockSpec(gm.shape, lambda b, i: (0, 0))],
        out_specs=pl.BlockSpec((ts, HGRN_WIDTH), lambda b, i: (b * nt + i, 0)),
        out_shape=jax.ShapeDtypeStruct((batch * seq, HGRN_WIDTH), BF16),
        scratch_shapes=[pltpu.VMEM((HGRN_HEADS, HGRN_HEAD_DIM, HGRN_HEAD_DIM), F32)],
        compiler_params=_params(("arbitrary", "arbitrary")),
        name="hgrn_mixer",
    )(h32, h32, h32, h32, lb_logits, norm_g, gm)


def _tri_ij(t, n):
    i = sum((t >= r * (r + 1) // 2).astype(jnp.int32) for r in range(1, n)) if n > 1 else t * 0
    return i, t - i * (i + 1) // 2


def _fox_kernel(q_ref, k_ref, vt_ref, qa_ref, ka_ref, o_ref, m_ref, acc_ref, *, n_q):
    tq = q_ref.shape[0]
    tk = k_ref.shape[0]
    i, j = _tri_ij(pl.program_id(1), n_q)
    dh = FOX_HEAD_DIM

    @pl.when(j == 0)
    def _():
        m_ref[...] = jnp.full(m_ref.shape, MASK_VALUE, F32)
        acc_ref[...] = jnp.zeros_like(acc_ref)

    def kv_step(on_diagonal):
        if on_diagonal:
            causal = (lax.broadcasted_iota(jnp.int32, (tk, tq), 1)
                      >= lax.broadcasted_iota(jnp.int32, (tk, tq), 0))
        ones = jnp.ones((FOX_SUM_ROWS, tk), BF16)
        qa = qa_ref[...]
        ka = ka_ref[...]
        aug_head = lax.broadcasted_iota(jnp.int32, ka.shape, 1) // FOX_AUG_LANES
        scores, col_max, probs, alphas = {}, {}, {}, {}

        def score_stage(h):
            lanes = slice(h * dh, (h + 1) * dh)
            qc = jnp.concatenate([q_ref[:, lanes], qa], axis=1)
            kc = jnp.concatenate([k_ref[:, lanes], jnp.where(aug_head == h, ka, jnp.zeros_like(ka))], axis=1)
            s_t = lax.dot_general(kc, qc, DOT_NT, preferred_element_type=F32)
            if on_diagonal:
                s_t = jnp.where(causal, s_t, MASK_VALUE)
            scores[h] = s_t
            col_max[h] = jnp.max(s_t, axis=0, keepdims=True)

        def softmax_stage(h):
            m_prev = m_ref[h]
            m_new = jnp.maximum(m_prev, jnp.broadcast_to(col_max[h], m_prev.shape))
            alphas[h] = jnp.exp2(m_prev - m_new)[0:1, :]
            probs[h] = jnp.exp2(scores[h] - m_new[0:1, :]).astype(BF16)
            m_ref[h] = m_new

        def value_stage(h):
            lhs = jnp.concatenate([vt_ref[h * dh:(h + 1) * dh, :], ones], axis=0)
            acc_ref[h] = alphas[h] * acc_ref[h] + jnp.dot(lhs, probs[h], preferred_element_type=F32)

        for stage in (score_stage, softmax_stage, value_stage):
            for h in range(FOX_HEADS):
                stage(h)

    @pl.when(j < i)
    def _():
        kv_step(False)

    @pl.when(j == i)
    def _():
        kv_step(True)
        for h in range(FOX_HEADS):
            acc = acc_ref[h]
            o_ref[h * dh:(h + 1) * dh, :] = (acc[0:dh, :] / acc[dh:dh + 1, :]).astype(BF16)


def _fox(qk, vt, qa, ka, batch, seq):
    tq = TQ_FOX
    nq = seq // tq
    q_blk = lambda b, t: b * nq + _tri_ij(t, nq)[0]
    kv_blk = lambda b, t: b * nq + _tri_ij(t, nq)[1]
    return pl.pallas_call(
        functools.partial(_fox_kernel, n_q=nq),
        grid=(batch, nq * (nq + 1) // 2),
        in_specs=[pl.BlockSpec((tq, FOX_WIDTH), lambda b, t: (q_blk(b, t), 0)),
                  pl.BlockSpec((tq, FOX_WIDTH), lambda b, t: (kv_blk(b, t), 1)),
                  pl.BlockSpec((FOX_WIDTH, tq), lambda b, t: (0, kv_blk(b, t))),
                  pl.BlockSpec((tq, LANE), lambda b, t: (q_blk(b, t), 0)),
                  pl.BlockSpec((tq, LANE), lambda b, t: (kv_blk(b, t), 0))],
        out_specs=pl.BlockSpec((FOX_WIDTH, tq), lambda b, t: (0, q_blk(b, t))),
        out_shape=jax.ShapeDtypeStruct((FOX_WIDTH, batch * seq), BF16),
        scratch_shapes=[pltpu.VMEM((FOX_HEADS, SUBLANES, tq), F32),
                        pltpu.VMEM((FOX_HEADS, FOX_HEAD_DIM + FOX_SUM_ROWS, tq), F32)],
        compiler_params=_params(("arbitrary", "arbitrary")),
        name="fox_attention",
    )(qk, qk, vt, qa, ka)


def _layer_norm(z, g, b):
    mu = jnp.mean(z, axis=-1, keepdims=True)
    zc = z - mu
    var = jnp.mean(zc * zc, axis=-1, keepdims=True)
    return zc * lax.rsqrt(var + LN_EPS) * g + b


def _outproj_kernel(yp_ref, yh_ref, yf_ref, w_ref, x_ref, g_ref, b_ref, o32_ref, o16_ref):
    tm = x_ref.shape[0]
    sub = min(tm, SUB_LN)
    n_tok = POOL_WIDTH + HGRN_WIDTH
    for r0 in range(0, tm, sub):
        rows = slice(r0, r0 + sub)
        y = jnp.concatenate([yp_ref[rows, :], yh_ref[rows, :]], axis=1)
        z = (DEEPNORM_ALPHA * x_ref[rows, :]
             + jnp.dot(y, w_ref[0:n_tok, :], preferred_element_type=F32)
             + lax.dot_general(yf_ref[:, rows], w_ref[n_tok:, :], DOT_TN, preferred_element_type=F32))
        out = _layer_norm(z, g_ref[...], b_ref[...])
        o32_ref[rows, :] = out
        o16_ref[rows, :] = out.astype(BF16)


def _outproj(yp, yh, yf, w, x, g, b):
    m = x.shape[0]
    tm = TM_OUTPROJ
    row = lambda width: pl.BlockSpec((tm, width), lambda i: (i, 0))
    const = lambda shape: pl.BlockSpec(shape, lambda i: (0, 0))
    return pl.pallas_call(
        _outproj_kernel,
        grid=(m // tm,),
        in_specs=[row(POOL_WIDTH), row(HGRN_WIDTH), pl.BlockSpec((FOX_WIDTH, tm), lambda i: (0, i)),
                  const((D_MODEL, D_MODEL)), row(D_MODEL), const((1, D_MODEL)), const((1, D_MODEL))],
        out_specs=[row(D_MODEL), row(D_MODEL)],
        out_shape=[jax.ShapeDtypeStruct((m, D_MODEL), F32), jax.ShapeDtypeStruct((m, D_MODEL), BF16)],
        compiler_params=_params(("arbitrary",)),
        name="outproj_ln",
    )(yp, yh, yf, w, x, g, b)


def _ffn_up_kernel(x_ref, wg_ref, wv_ref, cw_ref, cb_ref, o_ref, gbuf_ref, wgb_ref, wvb_ref, *, tiles_per_seq):
    tm = x_ref.shape[0]
    sub = min(tm, SUB_FFN_UP)
    i = pl.program_id(1)

    @pl.when(i == 0)
    def _():
        wgb_ref[...] = wg_ref[...].astype(BF16)
        wvb_ref[...] = wv_ref[...].astype(BF16)

    @pl.when(i % tiles_per_seq == 0)
    def _():
        gbuf_ref[0:8, :] = jnp.zeros((8, gbuf_ref.shape[1]), F32)

    for r0 in range(0, tm, sub):
        x = x_ref[r0:r0 + sub, :]
        gate = jnp.dot(x, wgb_ref[...], preferred_element_type=F32)
        val = jnp.dot(x, wvb_ref[...], preferred_element_type=F32)
        gbuf_ref[8 + r0:8 + r0 + sub, :] = gate
        conv = (cw_ref[2:3, :] * gate + cw_ref[1:2, :] * gbuf_ref[7 + r0:7 + r0 + sub, :]
                + cw_ref[0:1, :] * gbuf_ref[6 + r0:6 + r0 + sub, :] + cb_ref[...])
        o_ref[r0:r0 + sub, :] = (conv / (1.0 + jnp.exp(-conv)) * val).astype(BF16)
    gbuf_ref[0:8, :] = gbuf_ref[tm:tm + 8, :]


def _ffn_up(xb, w_gate, w_val, layer, cw, cb, seq):
    m = xb.shape[0]
    tm, tn = TM_FFN_UP, TN_FFN_UP
    weight = pl.BlockSpec((None, D_MODEL, tn), lambda j, i: (layer, 0, j))
    return pl.pallas_call(
        functools.partial(_ffn_up_kernel, tiles_per_seq=seq // tm),
        grid=(D_FF // tn, m // tm),
        in_specs=[pl.BlockSpec((tm, D_MODEL), lambda j, i: (i, 0)), weight, weight,
                  pl.BlockSpec((3, tn), lambda j, i: (0, j)),
                  pl.BlockSpec((1, tn), lambda j, i: (0, j))],
        out_specs=pl.BlockSpec((tm, tn), lambda j, i: (i, j)),
        out_shape=jax.ShapeDtypeStruct((m, D_FF), BF16),
        scratch_shapes=[pltpu.VMEM((tm + 8, tn), F32),
                        pltpu.VMEM((D_MODEL, tn), BF16),
                        pltpu.VMEM((D_MODEL, tn), BF16)],
        compiler_params=_params(("arbitrary", "arbitrary")),
        name="ffn_up",
    )(xb, w_gate, w_val, cw, cb)


def _ffn_down_kernel(a_ref, w_ref, x_ref, g_ref, b_ref, o32_ref, *maybe_o16_ref):
    tm = x_ref.shape[0]
    sub = min(tm, SUB_LN)
    for r0 in range(0, tm, sub):
        rows = slice(r0, r0 + sub)
        z = DEEPNORM_ALPHA * x_ref[rows, :] + jnp.dot(a_ref[rows, :], w_ref[...], preferred_element_type=F32)
        out = _layer_norm(z, g_ref[...], b_ref[...])
        o32_ref[rows, :] = out
        for o16_ref in maybe_o16_ref:
            o16_ref[rows, :] = out.astype(BF16)


def _ffn_down(a, w, x, g, b, want_bf16):
    m = x.shape[0]
    tm = TM_FFN_DOWN
    row = lambda width: pl.BlockSpec((tm, width), lambda i: (i, 0))
    const = lambda shape: pl.BlockSpec(shape, lambda i: (0, 0))
    out_specs = [row(D_MODEL)]
    out_shape = [jax.ShapeDtypeStruct((m, D_MODEL), F32)]
    if want_bf16:
        out_specs.append(row(D_MODEL))
        out_shape.append(jax.ShapeDtypeStruct((m, D_MODEL), BF16))
    return pl.pallas_call(
        _ffn_down_kernel,
        grid=(m // tm,),
        in_specs=[row(D_FF),
                  pl.BlockSpec((D_FF, D_MODEL), lambda i: (0, 0), pipeline_mode=pl.Buffered(1)),
                  row(D_MODEL), const((1, D_MODEL)), const((1, D_MODEL))],
        out_specs=out_specs,
        out_shape=out_shape,
        compiler_params=_params(("arbitrary",)),
        name="ffn_down_ln",
    )(a, w, x, g, b)


def kernel(x, w_in, fox_f_bias, pool_w, pool_scale, hgrn_lb_logits, hgrn_norm_g, w_out, ln1_g, ln1_b,
           w_gate, w_val, conv_w, conv_b, w_down, ln2_g, ln2_b):
    batch, seq, _ = x.shape
    m = batch * seq
    x32 = x.reshape(m, D_MODEL)
    xb = x32
    w_in_t = jnp.swapaxes(w_in, 1, 2)
    q_end = H32_COLS + FOX_WIDTH
    for l in range(DEPTH):
        w32 = w_in_t[l, :H32_COLS].astype(BF16)
        k_end = q_end + FOX_WIDTH
        wqk = jnp.concatenate([w_in_t[l, H32_COLS:q_end] * FOX_Q_SCALE, w_in_t[l, q_end:k_end]],
                              axis=0).astype(BF16)
        wv = w_in_t[l, k_end:H32_COLS + H16_COLS].astype(BF16)
        wff = jnp.pad(w_in_t[l, H32_COLS + H16_COLS:], ((0, FOX_FF_ROWS - FOX_HEADS), (0, 0))).astype(BF16)
        bias_pad = jnp.pad(fox_f_bias[l], (0, FOX_FF_ROWS - FOX_HEADS)).reshape(FOX_FF_ROWS, 1)

        h32, yp, qk, vt, qa, ka = _inproj(xb, w32, wqk, wv, wff, bias_pad, pool_w[l].astype(BF16),
                                          pool_scale[l].reshape(1, POOL_WIDTH), seq)
        yh = _hgrn(h32, hgrn_lb_logits, hgrn_norm_g[l].reshape(1, HGRN_WIDTH), l, batch, seq)
        yf = _fox(qk, vt, qa, ka, batch, seq)
        x32, xb = _outproj(yp, yh, yf, w_out[l].astype(BF16), x32,
                           ln1_g[l].reshape(1, D_MODEL), ln1_b[l].reshape(1, D_MODEL))
        a = _ffn_up(xb, w_gate, w_val, l, conv_w[l], conv_b[l].reshape(1, D_FF), seq)
        outs = _ffn_down(a, w_down[l].astype(BF16), x32, ln2_g[l].reshape(1, D_MODEL),
                         ln2_b[l].reshape(1, D_MODEL), want_bf16=l + 1 < DEPTH)
        x32 = outs[0]
        if l + 1 < DEPTH:
            xb = outs[1]
    return x32.reshape(batch, seq, D_MODEL)
```

```python
import functools

import jax
import jax.numpy as jnp
import numpy as np
from jax import lax
from jax.experimental import pallas as pl
from jax.experimental.pallas import tpu as pltpu

F32 = jnp.float32
BF16 = jnp.bfloat16

D_MODEL = 2048
DEPTH = 4
POOL_WIDTH = 512
POOL_WINDOWS = (2, 4, 8, 16)
POOL_GROUP = 128
POOL_HALO = 16
HGRN_WIDTH = 512
HGRN_HEAD_DIM = 128
HGRN_HEADS = 4
FOX_WIDTH = 1024
FOX_HEAD_DIM = 128
FOX_HEADS = 8
D_FF = 5632
DEEPNORM_ALPHA = (2 * DEPTH) ** 0.25
LN_EPS = 1e-5
RMS_EPS = 1e-6
MASK_VALUE = -1e30
EXP_CLAMP = 80.0
LOG2_E = 1.4426950408889634
FOX_Q_SCALE = FOX_HEAD_DIM ** -0.5 * LOG2_E

H32_COLS = POOL_WIDTH + 4 * HGRN_WIDTH
H16_COLS = 3 * FOX_WIDTH
FOX_FF_ROWS = 16
FOX_AUG_LANES = 16
FOX_SUM_ROWS = 16

DOT_NT = (((1,), (1,)), ((), ()))
DOT_TN = (((0,), (0,)), ((), ()))
LANE = 128
SUBLANES = 8
VMEM_LIMIT = 58 * 1024 * 1024

TM_INPROJ = 512
TS_HGRN = 256
TQ_FOX = 512
TM_OUTPROJ = 512
TM_FFN_UP = 2048
TN_FFN_UP = 512
SUB_FFN_UP = 256
TM_FFN_DOWN = 512
SUB_LN = 128

HGRN_BLOCK = 128


def _params(sem):
    return pltpu.CompilerParams(dimension_semantics=sem, vmem_limit_bytes=VMEM_LIMIT)


def _log_sigmoid(x):
    return jnp.minimum(x, 0.0) - jnp.log1p(jnp.exp(-jnp.abs(x)))


def _split2(x):
    a = x.astype(BF16)
    b = (x - a.astype(F32)).astype(BF16)
    return jnp.concatenate([a, b], axis=1)


def _sum2(y, n):
    return y[:, 0:n] + y[:, n:2 * n]


def _inproj32_kernel(x_ref, w_ref, pw_ref, ps_ref, h_ref, yp_ref, ext_ref, *, tiles_per_seq):
    tm = x_ref.shape[0]
    i = pl.program_id(0)

    @pl.when(i % tiles_per_seq == 0)
    def _():
        ext_ref[0:POOL_HALO, :] = jnp.zeros((POOL_HALO, POOL_WIDTH), F32)

    x = x_ref[...].astype(BF16)
    u = lax.dot_general(x, w_ref[0:POOL_WIDTH, :], DOT_NT, preferred_element_type=F32)
    h_ref[...] = lax.dot_general(x, w_ref[POOL_WIDTH:, :], DOT_NT, preferred_element_type=F32)

    ext_ref[POOL_HALO:POOL_HALO + tm, :] = u
    pos = ((i % tiles_per_seq) * tm + lax.broadcasted_iota(jnp.int32, (tm, 1), 0)).astype(F32)
    for gi, w in enumerate(POOL_WINDOWS):
        lanes = slice(gi * POOL_GROUP, (gi + 1) * POOL_GROUP)
        ug = u[:, lanes]
        acc = ug
        for j in range(1, w):
            acc = acc + ext_ref[POOL_HALO - j:POOL_HALO - j + tm, lanes]
        count = jnp.minimum(pos + 1.0, float(w))
        d = (acc / count - ug).astype(BF16)
        y = jnp.dot(d, pw_ref[gi], preferred_element_type=F32) * ps_ref[:, lanes]
        yp_ref[:, lanes] = y.astype(BF16)
    ext_ref[0:POOL_HALO, :] = ext_ref[tm:tm + POOL_HALO, :]


def _split3_rows(x):
    a = x.astype(BF16)
    r = x - a.astype(F32)
    b = r.astype(BF16)
    c = (r - b.astype(F32)).astype(BF16)
    return jnp.concatenate([a, b, c], axis=0)


def _inproj16_kernel(x_ref, w_ref, wv_ref, wff_ref, bias_ref, triu_ref, sel_ref, ones_ref,
                     o_ref, vt_ref, qa_ref, ka_ref, carry_ref, *, tiles_per_seq):
    tm = x_ref.shape[0]
    nr = FOX_FF_ROWS

    @pl.when(pl.program_id(0) % tiles_per_seq == 0)
    def _():
        carry_ref[...] = jnp.zeros_like(carry_ref)

    x = x_ref[...].astype(BF16)
    ff_t = lax.dot_general(wff_ref[...], x, DOT_NT, preferred_element_type=F32)
    lf_t = _log_sigmoid(ff_t + bias_ref[...])
    c3 = jnp.dot(_split3_rows(lf_t), triu_ref[...], preferred_element_type=F32)
    c_t = c3[0:nr] + c3[nr:2 * nr] + c3[2 * nr:3 * nr] + carry_ref[:, 0:1]
    carry_ref[...] = jnp.broadcast_to(c_t[:, tm - 1:tm], carry_ref.shape)
    aug = lax.dot_general(_split3_rows(c_t * LOG2_E), sel_ref[...], DOT_TN,
                          preferred_element_type=F32) + ones_ref[...]
    qa_ref[...] = aug[:, :LANE].astype(BF16)
    ka_ref[...] = aug[:, LANE:].astype(BF16)
    o_ref[...] = lax.dot_general(x, w_ref[...], DOT_NT, preferred_element_type=F32).astype(BF16)
    vt_ref[...] = lax.dot_general(wv_ref[...], x, DOT_NT, preferred_element_type=F32).astype(BF16)


def _inproj(xb, w32, wqk, wv, wff, bias_pad, pool_w, pool_scale, seq):
    m = xb.shape[0]
    tm = TM_INPROJ
    row = lambda width: pl.BlockSpec((tm, width), lambda i: (i, 0))
    const = lambda shape: pl.BlockSpec(shape, lambda i: (0,) * len(shape))
    h32, yp = pl.pallas_call(
        functools.partial(_inproj32_kernel, tiles_per_seq=seq // tm),
        grid=(m // tm,),
        in_specs=[row(D_MODEL), const((H32_COLS, D_MODEL)),
                  const((len(POOL_WINDOWS), POOL_GROUP, POOL_GROUP)), const((1, POOL_WIDTH))],
        out_specs=[row(4 * HGRN_WIDTH), row(POOL_WIDTH)],
        out_shape=[jax.ShapeDtypeStruct((m, 4 * HGRN_WIDTH), F32),
                   jax.ShapeDtypeStruct((m, POOL_WIDTH), BF16)],
        scratch_shapes=[pltpu.VMEM((tm + POOL_HALO, POOL_WIDTH), F32)],
        compiler_params=_params(("arbitrary",)),
        name="inproj32_pool",
    )(xb, w32, pool_w, pool_scale)
    sel, ones = _fox_aug_constants()
    triu = np.triu(np.ones((tm, tm), np.float32))
    qk, vt, qa, ka = pl.pallas_call(
        functools.partial(_inproj16_kernel, tiles_per_seq=seq // tm),
        grid=(m // tm,),
        in_specs=[row(D_MODEL), const((2 * FOX_WIDTH, D_MODEL)), const((FOX_WIDTH, D_MODEL)),
                  const((FOX_FF_ROWS, D_MODEL)), const((FOX_FF_ROWS, tm)), const(triu.shape),
                  const(sel.shape), const(ones.shape)],
        out_specs=[row(2 * FOX_WIDTH), pl.BlockSpec((FOX_WIDTH, tm), lambda i: (0, i)), row(LANE), row(LANE)],
        out_shape=[jax.ShapeDtypeStruct((m, 2 * FOX_WIDTH), BF16),
                   jax.ShapeDtypeStruct((FOX_WIDTH, m), BF16),
                   jax.ShapeDtypeStruct((m, LANE), BF16),
                   jax.ShapeDtypeStruct((m, LANE), BF16)],
        scratch_shapes=[pltpu.VMEM((FOX_FF_ROWS, LANE), F32)],
        compiler_params=_params(("arbitrary",)),
        name="inproj16_cumsum",
    )(xb, wqk, wv, wff, jnp.broadcast_to(bias_pad, (FOX_FF_ROWS, tm)), jnp.asarray(triu, dtype=BF16),
      jnp.asarray(sel, dtype=BF16), jnp.asarray(ones))
    return h32, yp, qk, vt, qa, ka


def _fox_aug_constants():
    sel = np.zeros((3 * FOX_FF_ROWS, 2 * LANE), np.float32)
    ones = np.zeros((1, 2 * LANE), np.float32)
    for h in range(FOX_HEADS):
        for p in range(3):
            sel[p * FOX_FF_ROWS + h, h * FOX_AUG_LANES + p] = 1.0
            sel[p * FOX_FF_ROWS + h, LANE + h * FOX_AUG_LANES + 3 + p] = -1.0
            ones[0, h * FOX_AUG_LANES + 3 + p] = 1.0
            ones[0, LANE + h * FOX_AUG_LANES + p] = 1.0
    return sel, ones


def _hgrn_levels(c):
    return tuple(c >> k for k in range(1, c.bit_length()))


HGRN_MATRIX_LEVELS = (4, 2)


def _hgrn_level_matrix(c):
    t = np.arange(c)[:, None]
    u = np.arange(c)[None, :]
    mats = [(u <= t)]
    for m in HGRN_MATRIX_LEVELS:
        p = t % (2 * m)
        r = t - p + m - 1
        upper = p >= m
        mats.append(np.where(upper, (u > r) & (u <= t), (u > t) & (u <= r)))
    return np.concatenate(mats, axis=0).astype(np.float32)


def _hgrn_kernel(q_ref, z_ref, v_ref, g_ref, lbl_ref, ng_ref, gm_ref, o_ref, st_ref, *, layer):
    c = q_ref.shape[0]
    d = HGRN_HEAD_DIM
    blk = min(c, HGRN_BLOCK)
    nblk = c // blk
    levels = _hgrn_levels(c)

    @pl.when(pl.program_id(1) == 0)
    def _():
        st_ref[...] = jnp.zeros_like(st_ref)

    lbl = lbl_ref[...]
    e = jnp.exp(lbl - jnp.max(lbl, axis=0, keepdims=True))
    lbw = e / jnp.sum(e, axis=0, keepdims=True)
    lb_all = jnp.zeros((1, HGRN_WIDTH), F32)
    for li in range(1, layer + 1):
        lb_all = lb_all + lbw[li:li + 1, :]

    gm = gm_ref[...]
    row_d = lax.broadcasted_iota(jnp.int32, (c, d), 0)
    row_b = lax.broadcasted_iota(jnp.int32, (blk, blk), 0)
    col_b = lax.broadcasted_iota(jnp.int32, (blk, blk), 1)
    same = row_b ^ col_b
    dn_t = (((1,), (1,)), ((), ()))
    dn_a = (((0,), (0,)), ((), ()))
    rows = lambda bi: slice(bi * blk, (bi + 1) * blk)

    heads = range(HGRN_HEADS)
    lane = lambda h: slice(h * d, (h + 1) * d)
    q = [q_ref[:, lane(h)] for h in heads]
    v = [v_ref[:, lane(h)].astype(BF16) for h in heads]

    kk, lf2 = [], []
    for h in heads:
        z = z_ref[:, lane(h)]
        log_f = jnp.minimum(z, 0.0) - jnp.log(1.0 + jnp.exp(-jnp.abs(z)))
        sig_neg = 1.0 / (1.0 + jnp.exp(z))
        if layer > 0:
            lb = lb_all[:, lane(h)]
            log_f = log_f + jnp.log(1.0 + lb * jnp.exp(jnp.minimum(-z, EXP_CLAMP)))
            sig_neg = (1.0 - lb) * sig_neg
        kk.append(sig_neg)
        lf2.append(jnp.minimum(log_f, 0.0) * LOG2_E)

    ex = [_sum2(jnp.dot(gm, _split2(lf2[h]), preferred_element_type=F32), d) for h in heads]

    o_inter = []
    for h in heads:
        b = ex[h][0:c]
        b_last = b[c - 1:c, :]
        st = st_ref[h]
        qi = (q[h] * jnp.exp2(b)).astype(BF16)
        o_inter.append(lax.dot_general(qi, st.astype(BF16), dn_t, preferred_element_type=F32))
        kd = (kk[h] * jnp.exp2(b_last - b)).astype(BF16)
        st_ref[h] = st * jnp.exp2(b_last) + lax.dot_general(v[h], kd, dn_a, preferred_element_type=F32)

    diag = [[None] * nblk for _ in heads]
    off = [{} for _ in heads]
    for m in levels:
        upper = (row_d & m) != 0
        for h in heads:
            b = ex[h][0:c]
            if m >= SUBLANES:
                parts = []
                for r0 in range(0, c, 2 * m):
                    lo, hi = slice(r0, r0 + m), slice(r0 + m, r0 + 2 * m)
                    b_ref_row = b[r0 + m - 1:r0 + m, :]
                    parts.append(kk[h][lo] * jnp.exp2(b_ref_row - b[lo]))
                    parts.append(q[h][hi] * jnp.exp2(b[hi] - b_ref_row))
                x = jnp.concatenate(parts, axis=0).astype(BF16)
            elif m == 1:
                x = jnp.where(upper, q[h] * jnp.exp2(lf2[h]), kk[h]).astype(BF16)
            else:
                k = 1 + HGRN_MATRIX_LEVELS.index(m)
                x = (jnp.where(upper, q[h], kk[h]) * jnp.exp2(ex[h][k * c:(k + 1) * c])).astype(BF16)
            for bi in range(nblk):
                if m < blk:
                    a = lax.dot_general(x[rows(bi)], x[rows(bi)], dn_t, preferred_element_type=F32)
                    diag[h][bi] = a if diag[h][bi] is None else jnp.where(same < 2 * m, a, diag[h][bi])
                elif (bi * blk) & m:
                    for bj in range(nblk):
                        if not (bj * blk) & m and (bi * blk) // (2 * m) == (bj * blk) // (2 * m):
                            off[h][bi, bj] = lax.dot_general(x[rows(bi)], x[rows(bj)], dn_t,
                                                             preferred_element_type=F32)

    for h in heads:
        qb = q[h].astype(BF16)
        kb = kk[h].astype(BF16)
        g = g_ref[:, lane(h)]
        gate = ng_ref[:, lane(h)] * (g / (1.0 + jnp.exp(-g)))
        for bi in range(nblk):
            a = lax.dot_general(qb[rows(bi)], kb[rows(bi)], dn_t, preferred_element_type=F32)
            att = jnp.where(same == 0, a, diag[h][bi])
            att = jnp.where(row_b >= col_b, att, 0.0)
            o = o_inter[h][rows(bi)] + jnp.dot(att.astype(BF16), v[h][rows(bi)], preferred_element_type=F32)
            for bj in range(bi):
                o = o + jnp.dot(off[h][bi, bj].astype(BF16), v[h][rows(bj)], preferred_element_type=F32)
            o = o * lax.rsqrt(jnp.mean(o * o, axis=-1, keepdims=True) + RMS_EPS)
            o_ref[rows(bi), lane(h)] = (o * gate[rows(bi)]).astype(BF16)


def _hgrn(h32, lb_logits, norm_g, layer, batch, seq):
    ts = TS_HGRN
    nt = seq // ts
    gm = jnp.asarray(_hgrn_level_matrix(ts), dtype=BF16)

    def col(k):
        return pl.BlockSpec((ts, HGRN_WIDTH), lambda b, i: (b * nt + i, k))

    return pl.pallas_call(
        functools.partial(_hgrn_kernel, layer=layer),
        grid=(batch, nt),
        in_specs=[col(0), col(1), col(2), col(3),
                  pl.BlockSpec((DEPTH, HGRN_WIDTH), lambda b, i: (0, 0)),
                  pl.BlockSpec((1, HGRN_WIDTH), lambda b, i: (0, 0)),
                  pl.BlockSpec(gm.shape, lambda b, i: (0, 0))],
        out_specs=pl.BlockSpec((ts, HGRN_WIDTH), lambda b, i: (b * nt + i, 0)),
        out_shape=jax.ShapeDtypeStruct((batch * seq, HGRN_WIDTH), BF16),
        scratch_shapes=[pltpu.VMEM((HGRN_HEADS, HGRN_HEAD_DIM, HGRN_HEAD_DIM), F32)],
        compiler_params=_params(("arbitrary", "arbitrary")),
        name="hgrn_mixer",
    )(h32, h32, h32, h32, lb_logits, norm_g, gm)


def _tri_ij(t, n):
    i = sum((t >= r * (r + 1) // 2).astype(jnp.int32) for r in range(1, n)) if n > 1 else t * 0
    return i, t - i * (i + 1) // 2


def _fox_kernel(q_ref, k_ref, vt_ref, qa_ref, ka_ref, o_ref, m_ref, acc_ref, *, n_q):
    tq = q_ref.shape[0]
    tk = k_ref.shape[0]
    i, j = _tri_ij(pl.program_id(1), n_q)
    dh = FOX_HEAD_DIM

    @pl.when(j == 0)
    def _():
        m_ref[...] = jnp.full(m_ref.shape, MASK_VALUE, F32)
        acc_ref[...] = jnp.zeros_like(acc_ref)

    def kv_step(on_diagonal):
        if on_diagonal:
            causal = (lax.broadcasted_iota(jnp.int32, (tk, tq), 1)
                      >= lax.broadcasted_iota(jnp.int32, (tk, tq), 0))
        ones = jnp.ones((FOX_SUM_ROWS, tk), BF16)
        qa = qa_ref[...]
        ka = ka_ref[...]
        aug_head = lax.broadcasted_iota(jnp.int32, ka.shape, 1) // FOX_AUG_LANES
        scores, col_max, probs, alphas = {}, {}, {}, {}

        def score_stage(h):
            lanes = slice(h * dh, (h + 1) * dh)
            qc = jnp.concatenate([q_ref[:, lanes], qa], axis=1)
            kc = jnp.concatenate([k_ref[:, lanes], jnp.where(aug_head == h, ka, jnp.zeros_like(ka))], axis=1)
            s_t = lax.dot_general(kc, qc, DOT_NT, preferred_element_type=F32)
            if on_diagonal:
                s_t = jnp.where(causal, s_t, MASK_VALUE)
            scores[h] = s_t
            col_max[h] = jnp.max(s_t, axis=0, keepdims=True)

        def softmax_stage(h):
            m_prev = m_ref[h]
            m_new = jnp.maximum(m_prev, jnp.broadcast_to(col_max[h], m_prev.shape))
            alphas[h] = jnp.exp2(m_prev - m_new)[0:1, :]
            probs[h] = jnp.exp2(scores[h] - m_new[0:1, :]).astype(BF16)
            m_ref[h] = m_new

        def value_stage(h):
            lhs = jnp.concatenate([vt_ref[h * dh:(h + 1) * dh, :], ones], axis=0)
            acc_ref[h] = alphas[h] * acc_ref[h] + jnp.dot(lhs, probs[h], preferred_element_type=F32)

        for stage in (score_stage, softmax_stage, value_stage):
            for h in range(FOX_HEADS):
                stage(h)

    @pl.when(j < i)
    def _():
        kv_step(False)

    @pl.when(j == i)
    def _():
        kv_step(True)
        for h in range(FOX_HEADS):
            acc = acc_ref[h]
            o_ref[h * dh:(h + 1) * dh, :] = (acc[0:dh, :] / acc[dh:dh + 1, :]).astype(BF16)


def _fox(qk, vt, qa, ka, batch, seq):
    tq = TQ_FOX
    nq = seq // tq
    q_blk = lambda b, t: b * nq + _tri_ij(t, nq)[0]
    kv_blk = lambda b, t: b * nq + _tri_ij(t, nq)[1]
    return pl.pallas_call(
        functools.partial(_fox_kernel, n_q=nq),
        grid=(batch, nq * (nq + 1) // 2),
        in_specs=[pl.BlockSpec((tq, FOX_WIDTH), lambda b, t: (q_blk(b, t), 0)),
                  pl.BlockSpec((tq, FOX_WIDTH), lambda b, t: (kv_blk(b, t), 1)),
                  pl.BlockSpec((FOX_WIDTH, tq), lambda b, t: (0, kv_blk(b, t))),
                  pl.BlockSpec((tq, LANE), lambda b, t: (q_blk(b, t), 0)),
                  pl.BlockSpec((tq, LANE), lambda b, t: (kv_blk(b, t), 0))],
        out_specs=pl.BlockSpec((FOX_WIDTH, tq), lambda b, t: (0, q_blk(b, t))),
        out_shape=jax.ShapeDtypeStruct((FOX_WIDTH, batch * seq), BF16),
        scratch_shapes=[pltpu.VMEM((FOX_HEADS, SUBLANES, tq), F32),
                        pltpu.VMEM((FOX_HEADS, FOX_HEAD_DIM + FOX_SUM_ROWS, tq), F32)],
        compiler_params=_params(("arbitrary", "arbitrary")),
        name="fox_attention",
    )(qk, qk, vt, qa, ka)


def _layer_norm(z, g, b):
    mu = jnp.mean(z, axis=-1, keepdims=True)
    zc = z - mu
    var = jnp.mean(zc * zc, axis=-1, keepdims=True)
    return zc * lax.rsqrt(var + LN_EPS) * g + b


def _outproj_kernel(yp_ref, yh_ref, yf_ref, w_ref, x_ref, g_ref, b_ref, o32_ref, o16_ref):
    tm = x_ref.shape[0]
    sub = min(tm, SUB_LN)
    n_tok = POOL_WIDTH + HGRN_WIDTH
    for r0 in range(0, tm, sub):
        rows = slice(r0, r0 + sub)
        y = jnp.concatenate([yp_ref[rows, :], yh_ref[rows, :]], axis=1)
        z = (DEEPNORM_ALPHA * x_ref[rows, :]
             + jnp.dot(y, w_ref[0:n_tok, :], preferred_element_type=F32)
             + lax.dot_general(yf_ref[:, rows], w_ref[n_tok:, :], DOT_TN, preferred_element_type=F32))
        out = _layer_norm(z, g_ref[...], b_ref[...])
        o32_ref[rows, :] = out
        o16_ref[rows, :] = out.astype(BF16)


def _outproj(yp, yh, yf, w, x, g, b):
    m = x.shape[0]
    tm = TM_OUTPROJ
    row = lambda width: pl.BlockSpec((tm, width), lambda i: (i, 0))
    const = lambda shape: pl.BlockSpec(shape, lambda i: (0, 0))
    return pl.pallas_call(
        _outproj_kernel,
        grid=(m // tm,),
        in_specs=[row(POOL_WIDTH), row(HGRN_WIDTH), pl.BlockSpec((FOX_WIDTH, tm), lambda i: (0, i)),
                  const((D_MODEL, D_MODEL)), row(D_MODEL), const((1, D_MODEL)), const((1, D_MODEL))],
        out_specs=[row(D_MODEL), row(D_MODEL)],
        out_shape=[jax.ShapeDtypeStruct((m, D_MODEL), F32), jax.ShapeDtypeStruct((m, D_MODEL), BF16)],
        compiler_params=_params(("arbitrary",)),
        name="outproj_ln",
    )(yp, yh, yf, w, x, g, b)


def _ffn_up_kernel(x_ref, wg_ref, wv_ref, cw_ref, cb_ref, o_ref, gbuf_ref, wgb_ref, wvb_ref, *, tiles_per_seq):
    tm = x_ref.shape[0]
    sub = min(tm, SUB_FFN_UP)
    i = pl.program_id(1)

    @pl.when(i == 0)
    def _():
        wgb_ref[...] = wg_ref[...].astype(BF16)
        wvb_ref[...] = wv_ref[...].astype(BF16)

    @pl.when(i % tiles_per_seq == 0)
    def _():
        gbuf_ref[0:8, :] = jnp.zeros((8, gbuf_ref.shape[1]), F32)

    for r0 in range(0, tm, sub):
        x = x_ref[r0:r0 + sub, :]
        gate = jnp.dot(x, wgb_ref[...], preferred_element_type=F32)
        val = jnp.dot(x, wvb_ref[...], preferred_element_type=F32)
        gbuf_ref[8 + r0:8 + r0 + sub, :] = gate
        conv = (cw_ref[2:3, :] * gate + cw_ref[1:2, :] * gbuf_ref[7 + r0:7 + r0 + sub, :]
                + cw_ref[0:1, :] * gbuf_ref[6 + r0:6 + r0 + sub, :] + cb_ref[...])
        o_ref[r0:r0 + sub, :] = (conv / (1.0 + jnp.exp(-conv)) * val).astype(BF16)
    gbuf_ref[0:8, :] = gbuf_ref[tm:tm + 8, :]


def _ffn_up(xb, w_gate, w_val, layer, cw, cb, seq):
    m = xb.shape[0]
    tm, tn = TM_FFN_UP, TN_FFN_UP
    weight = pl.BlockSpec((None, D_MODEL, tn), lambda j, i: (layer, 0, j))
    return pl.pallas_call(
        functools.partial(_ffn_up_kernel, tiles_per_seq=seq // tm),
        grid=(D_FF // tn, m // tm),
        in_specs=[pl.BlockSpec((tm, D_MODEL), lambda j, i: (i, 0)), weight, weight,
                  pl.BlockSpec((3, tn), lambda j, i: (0, j)),
                  pl.BlockSpec((1, tn), lambda j, i: (0, j))],
        out_specs=pl.BlockSpec((tm, tn), lambda j, i: (i, j)),
        out_shape=jax.ShapeDtypeStruct((m, D_FF), BF16),
        scratch_shapes=[pltpu.VMEM((tm + 8, tn), F32),
                        pltpu.VMEM((D_MODEL, tn), BF16),
                        pltpu.VMEM((D_MODEL, tn), BF16)],
        compiler_params=_params(("arbitrary", "arbitrary")),
        name="ffn_up",
    )(xb, w_gate, w_val, cw, cb)


def _ffn_down_kernel(a_ref, w_ref, x_ref, g_ref, b_ref, o32_ref, *maybe_o16_ref):
    tm = x_ref.shape[0]
    sub = min(tm, SUB_LN)
    for r0 in range(0, tm, sub):
        rows = slice(r0, r0 + sub)
        z = DEEPNORM_ALPHA * x_ref[rows, :] + jnp.dot(a_ref[rows, :], w_ref[...], preferred_element_type=F32)
        out = _layer_norm(z, g_ref[...], b_ref[...])
        o32_ref[rows, :] = out
        for o16_ref in maybe_o16_ref:
            o16_ref[rows, :] = out.astype(BF16)


def _ffn_down(a, w, x, g, b, want_bf16):
    m = x.shape[0]
    tm = TM_FFN_DOWN
    row = lambda width: pl.BlockSpec((tm, width), lambda i: (i, 0))
    const = lambda shape: pl.BlockSpec(shape, lambda i: (0, 0))
    out_specs = [row(D_MODEL)]
    out_shape = [jax.ShapeDtypeStruct((m, D_MODEL), F32)]
    if want_bf16:
        out_specs.append(row(D_MODEL))
        out_shape.append(jax.ShapeDtypeStruct((m, D_MODEL), BF16))
    return pl.pallas_call(
        _ffn_down_kernel,
        grid=(m // tm,),
        in_specs=[row(D_FF),
                  pl.BlockSpec((D_FF, D_MODEL), lambda i: (0, 0), pipeline_mode=pl.Buffered(1)),
                  row(D_MODEL), const((1, D_MODEL)), const((1, D_MODEL))],
        out_specs=out_specs,
        out_shape=out_shape,
        compiler_params=_params(("arbitrary",)),
        name="ffn_down_ln",
    )(a, w, x, g, b)


def kernel(x, w_in, fox_f_bias, pool_w, pool_scale, hgrn_lb_logits, hgrn_norm_g, w_out, ln1_g, ln1_b,
           w_gate, w_val, conv_w, conv_b, w_down, ln2_g, ln2_b):
    batch, seq, _ = x.shape
    m = batch * seq
    x32 = x.reshape(m, D_MODEL)
    xb = x32
    w_in_t = jnp.swapaxes(w_in, 1, 2)
    q_end = H32_COLS + FOX_WIDTH
    for l in range(DEPTH):
        w32 = w_in_t[l, :H32_COLS].astype(BF16)
        k_end = q_end + FOX_WIDTH
        wqk = jnp.concatenate([w_in_t[l, H32_COLS:q_end] * FOX_Q_SCALE, w_in_t[l, q_end:k_end]],
                              axis=0).astype(BF16)
        wv = w_in_t[l, k_end:H32_COLS + H16_COLS].astype(BF16)
        wff = jnp.pad(w_in_t[l, H32_COLS + H16_COLS:], ((0, FOX_FF_ROWS - FOX_HEADS), (0, 0))).astype(BF16)
        bias_pad = jnp.pad(fox_f_bias[l], (0, FOX_FF_ROWS - FOX_HEADS)).reshape(FOX_FF_ROWS, 1)

        h32, yp, qk, vt, qa, ka = _inproj(xb, w32, wqk, wv, wff, bias_pad, pool_w[l].astype(BF16),
                                          pool_scale[l].reshape(1, POOL_WIDTH), seq)
        yh = _hgrn(h32, hgrn_lb_logits, hgrn_norm_g[l].reshape(1, HGRN_WIDTH), l, batch, seq)
        yf = _fox(qk, vt, qa, ka, batch, seq)
        x32, xb = _outproj(yp, yh, yf, w_out[l].astype(BF16), x32,
                           ln1_g[l].reshape(1, D_MODEL), ln1_b[l].reshape(1, D_MODEL))
        a = _ffn_up(xb, w_gate, w_val, l, conv_w[l], conv_b[l].reshape(1, D_FF), seq)
        outs = _ffn_down(a, w_down[l].astype(BF16), x32, ln2_g[l].reshape(1, D_MODEL),
                         ln2_b[l].reshape(1, D_MODEL), want_bf16=l + 1 < DEPTH)
        x32 = outs[0]
        if l + 1 < DEPTH:
            xb = outs[1]
    return x32.reshape(batch, seq, D_MODEL)
```
